```python
import math
import jax, jax.numpy as jnp
from jax import lax
import numpy as np

D_MODEL = 1024
BATCH = 8
SEQ = 4096
DEPTH = 2
DEC_BATCH = 128
DEC_SEQ = 1
PAST_LEN = 16384
PAGE_SIZE = 128

N_HEADS = 8
Q_RANK = 384
KV_RANK = 256
NOPE_DIM = 64
ROPE_DIM = 32
V_DIM = 64
ROPE_THETA = 10000.0
MLA_WIDTH = N_HEADS * V_DIM
MLA_SCALE = 1.0 / math.sqrt(NOPE_DIM + ROPE_DIM)
Q_BLOCK = 128
SC_WIDTH = 512
SC_K = 3
CF_WIDTH = 512
CF_K = 31
N_MEM = 256
MEM_HEADS = 4
MEM_HEAD_DIM = D_MODEL // MEM_HEADS
MEM_SCALE = 1.0 / math.sqrt(MEM_HEAD_DIM)
D_FF = 2816
FFN_K = 3
N_BRANCH = 3
IN_SIZES = (Q_RANK, KV_RANK, ROPE_DIM, SC_WIDTH, SC_WIDTH, SC_WIDTH, CF_WIDTH, CF_WIDTH, N_BRANCH * D_MODEL)
D_IN = Q_RANK + KV_RANK + ROPE_DIM + 3 * SC_WIDTH + 2 * CF_WIDTH + N_BRANCH * D_MODEL
EPS = 1e-6

kernel_name = "hybrid_mla_shortconv_conformer_decoder_step"


def _rmsnorm(x, g):
    xf = x.astype(jnp.float32)
    y = xf * lax.rsqrt(jnp.mean(xf * xf, axis=-1, keepdims=True) + EPS)
    return (y * g.astype(jnp.float32)).astype(x.dtype)


def _layernorm(x, g, b):
    xf = x.astype(jnp.float32)
    mu = jnp.mean(xf, axis=-1, keepdims=True)
    var = jnp.mean(jnp.square(xf - mu), axis=-1, keepdims=True)
    y = (xf - mu) * lax.rsqrt(var + EPS)
    return (y * g.astype(jnp.float32) + b.astype(jnp.float32)).astype(x.dtype)


def _rope_tables(pos):
    half = ROPE_DIM // 2
    inv = ROPE_THETA ** (-jnp.arange(half, dtype=jnp.float32) / half)
    ang = pos.astype(jnp.float32)[:, None] * inv[None, :]
    return jnp.cos(ang), jnp.sin(ang)


def _apply_rope(x, cos, sin):
    half = ROPE_DIM // 2
    xf = x.astype(jnp.float32)
    x1, x2 = xf[..., :half], xf[..., half:]
    return jnp.concatenate([x1 * cos - x2 * sin, x1 * sin + x2 * cos], axis=-1).astype(x.dtype)


def _causal_dwconv(x, hist, w):
    xx = jnp.concatenate([hist.astype(x.dtype), x], axis=1)
    y = lax.conv_general_dilated(xx, w[:, None, :].astype(x.dtype), window_strides=(1,), padding='VALID',
                                 dimension_numbers=('NWC', 'WIO', 'NWC'), feature_group_count=x.shape[-1])
    return y, xx[:, xx.shape[1] - (w.shape[0] - 1):]


def _mla_prompt_attend(q_nope, q_rope, c, k_rope, w_uk, w_uv):
    B, S, H, _ = q_nope.shape
    k_nope = jnp.einsum('bsr,rhn->bshn', c, w_uk)
    v = jnp.einsum('bsr,rhv->bshv', c, w_uv)
    nb = S // Q_BLOCK
    qn = q_nope.reshape(B, nb, Q_BLOCK, H, NOPE_DIM).transpose(1, 0, 2, 3, 4)
    qr = q_rope.reshape(B, nb, Q_BLOCK, H, ROPE_DIM).transpose(1, 0, 2, 3, 4)
    kpos = jnp.arange(S)

    def block(args):
        qn_b, qr_b, i = args
        s = (jnp.einsum('bqhn,bkhn->bhqk', qn_b, k_nope) + jnp.einsum('bqhd,bkd->bhqk', qr_b, k_rope))
        s = s.astype(jnp.float32) * MLA_SCALE
        qpos = i * Q_BLOCK + jnp.arange(Q_BLOCK)
        s = jnp.where(kpos[None, :] <= qpos[:, None], s, -jnp.inf)
        p = jax.nn.softmax(s, axis=-1).astype(v.dtype)
        return jnp.einsum('bhqk,bkhv->bqhv', p, v)

    o = lax.map(block, (qn, qr, jnp.arange(nb)))
    return o.transpose(1, 0, 2, 3, 4).reshape(B, S, H, V_DIM)


def _make_sample_attend(c_past, kr_past):
    def attend(q_nope, q_rope, c, k_rope, w_uk, w_uv):
        T = q_nope.shape[1]
        L = c_past.shape[1]
        q_lat = jnp.einsum('bthn,rhn->bthr', q_nope, w_uk)
        s_past = jnp.einsum('bthr,blr->bhtl', q_lat, c_past) + jnp.einsum('bthd,bld->bhtl', q_rope, kr_past)
        s_new = jnp.einsum('bthr,bur->bhtu', q_lat, c) + jnp.einsum('bthd,bud->bhtu', q_rope, k_rope)
        causal = jnp.arange(T)[None, :] <= jnp.arange(T)[:, None]
        s_new = jnp.where(causal, s_new.astype(jnp.float32), -jnp.inf)
        s = jnp.concatenate([s_past.astype(jnp.float32), s_new], axis=-1) * MLA_SCALE
        p = jax.nn.softmax(s, axis=-1).astype(c.dtype)
        o_lat = (jnp.einsum('bhtl,blr->bthr', p[..., :L], c_past) + jnp.einsum('bhtu,bur->bthr', p[..., L:], c))
        return jnp.einsum('bthr,rhv->bthv', o_lat, w_uv)
    return attend


def _mix_block(h, pos, sc_hist, cf_hist, lw, attend):
    B, T, _ = h.shape
    z = h @ lw['w_in']
    offs = np.cumsum(IN_SIZES)[:-1].tolist()
    q_lat, kv_lat, k_r, sc_b, sc_c, sc_x, cf_a, cf_g, gates = jnp.split(z, offs, axis=-1)
    q = (_rmsnorm(q_lat, lw['g_q']) @ lw['w_uq']).reshape(B, T, N_HEADS, NOPE_DIM + ROPE_DIM)
    cos, sin = _rope_tables(pos)
    q_nope = q[..., :NOPE_DIM]
    q_rope = _apply_rope(q[..., NOPE_DIM:], cos[:, None, :], sin[:, None, :])
    c = _rmsnorm(kv_lat, lw['g_kv'])
    k_rope = _apply_rope(k_r, cos, sin)
    o = attend(q_nope, q_rope, c, k_rope, lw['w_uk'], lw['w_uv'])
    br_a = o.reshape(B, T, MLA_WIDTH) @ lw['w_mla_o']
    uc, sc_new = _causal_dwconv(sc_c * sc_x, sc_hist, lw['sc_conv_w'])
    br_b = (sc_b * uc) @ lw['w_sc_o']
    vc, cf_new = _causal_dwconv(cf_a * jax.nn.sigmoid(cf_g), cf_hist, lw['cf_conv_w'])
    br_c = jax.nn.silu(_layernorm(vc + lw['cf_conv_b'], lw['cf_ln_g'], lw['cf_ln_b'])) @ lw['w_cf_o']
    g = jax.nn.sigmoid(gates).reshape(B, T, N_BRANCH, D_MODEL)
    merged = g[..., 0, :] * br_a + g[..., 1, :] * br_b + g[..., 2, :] * br_c
    return merged @ lw['w_out'], c, k_rope, sc_new, cf_new


def _mem_kv(mem, g_src, w_mk, w_mv):
    B = mem.shape[0]
    m = _rmsnorm(mem, g_src)
    k = (m @ w_mk).reshape(B, N_MEM, MEM_HEADS, MEM_HEAD_DIM)
    v = (m @ w_mv).reshape(B, N_MEM, MEM_HEADS, MEM_HEAD_DIM)
    return k, v


def _mem_attend(h, k, v, w_mq, w_mo):
    B, T, _ = h.shape
    q = (h @ w_mq).reshape(B, T, MEM_HEADS, MEM_HEAD_DIM)
    s = jnp.einsum('bthd,bmhd->bhtm', q, k).astype(jnp.float32) * MEM_SCALE
    p = jax.nn.softmax(s, axis=-1).astype(v.dtype)
    o = jnp.einsum('bhtm,bmhd->bthd', p, v).reshape(B, T, D_MODEL)
    return o @ w_mo


def _conv_ffn(h, hist, w_up, conv_w, w_down):
    u = h @ w_up
    uc, new_hist = _causal_dwconv(u, hist, conv_w)
    gt, val = uc[..., :D_FF], uc[..., D_FF:]
    return (jax.nn.gelu(gt, approximate=True) * val) @ w_down, new_hist


def _layer(x, mem_k, mem_v, pos, sc_hist, cf_hist, ffn_hist, lw, attend):
    h = _rmsnorm(x, lw['g_mix_pre'])
    mixed, c, kr, sc_new, cf_new = _mix_block(h, pos, sc_hist, cf_hist, lw, attend)
    x = x + _rmsnorm(mixed, lw['g_mix_post'])
    h = _rmsnorm(x, lw['g_mem_pre'])
    x = x + _rmsnorm(_mem_attend(h, mem_k, mem_v, lw['w_mq'], lw['w_mo']), lw['g_mem_post'])
    h = _rmsnorm(x, lw['g_ffn_pre'])
    f, ffn_new = _conv_ffn(h, ffn_hist, lw['w_up'], lw['ffn_conv_w'], lw['w_down'])
    x = x + _rmsnorm(f, lw['g_ffn_post'])
    return x, c, kr, sc_new, cf_new, ffn_new


def setup_inputs(seed: int = 0) -> dict:
    key = jax.random.key(seed)
    ks = iter(jax.random.split(key, 64))
    f32 = jnp.float32

    def nrm(shape, scale=1.0):
        return jax.random.normal(next(ks), shape, f32) * scale

    def gain(n):
        return 1.0 + 0.1 * nrm((DEPTH, n))

    n_pages = PAST_LEN // PAGE_SIZE
    n_used = DEC_BATCH * n_pages
    n_pool = n_used + n_used // 4
    page_table = jax.random.permutation(next(ks), n_pool)[:n_used].reshape(DEC_BATCH, n_pages).astype(jnp.int32)
    return {
        'x_prompt': nrm((BATCH, SEQ, D_MODEL)),
        'x_sample': nrm((DEC_BATCH, DEC_SEQ, D_MODEL)),
        'mem_prompt': nrm((BATCH, N_MEM, D_MODEL)),
        'cache_kv_latent': nrm((DEPTH, n_pool, PAGE_SIZE, KV_RANK)),
        'cache_k_rope': nrm((DEPTH, n_pool, PAGE_SIZE, ROPE_DIM)),
        'cache_mem_k': nrm((DEPTH, DEC_BATCH, N_MEM, MEM_HEADS, MEM_HEAD_DIM)),
        'cache_mem_v': nrm((DEPTH, DEC_BATCH, N_MEM, MEM_HEADS, MEM_HEAD_DIM)),
        'state_sconv': nrm((DEPTH, DEC_BATCH, SC_K - 1, SC_WIDTH)),
        'state_conformer': nrm((DEPTH, DEC_BATCH, CF_K - 1, CF_WIDTH), 0.5),
        'state_ffn': nrm((DEPTH, DEC_BATCH, FFN_K - 1, 2 * D_FF)),
        'page_table': page_table,
        'g_mix_pre': gain(D_MODEL),
        'w_in': nrm((DEPTH, D_MODEL, D_IN), D_MODEL ** -0.5),
        'g_q': gain(Q_RANK),
        'w_uq': nrm((DEPTH, Q_RANK, N_HEADS * (NOPE_DIM + ROPE_DIM)), Q_RANK ** -0.5),
        'g_kv': gain(KV_RANK),
        'w_uk': nrm((DEPTH, KV_RANK, N_HEADS, NOPE_DIM), KV_RANK ** -0.5),
        'w_uv': nrm((DEPTH, KV_RANK, N_HEADS, V_DIM), KV_RANK ** -0.5),
        'w_mla_o': nrm((DEPTH, MLA_WIDTH, D_MODEL), MLA_WIDTH ** -0.5),
        'sc_conv_w': nrm((DEPTH, SC_K, SC_WIDTH), SC_K ** -0.5),
        'w_sc_o': nrm((DEPTH, SC_WIDTH, D_MODEL), SC_WIDTH ** -0.5),
        'cf_conv_w': nrm((DEPTH, CF_K, CF_WIDTH), CF_K ** -0.5),
        'cf_conv_b': nrm((DEPTH, CF_WIDTH), 0.02),
        'cf_ln_g': gain(CF_WIDTH),
        'cf_ln_b': nrm((DEPTH, CF_WIDTH), 0.02),
        'w_cf_o': nrm((DEPTH, CF_WIDTH, D_MODEL), CF_WIDTH ** -0.5),
        'w_out': nrm((DEPTH, D_MODEL, D_MODEL), D_MODEL ** -0.5),
        'g_mix_post': gain(D_MODEL),
        'g_mem_pre': gain(D_MODEL),
        'g_mem_src': gain(D_MODEL),
        'w_mq': nrm((DEPTH, D_MODEL, D_MODEL), D_MODEL ** -0.5),
        'w_mk': nrm((DEPTH, D_MODEL, D_MODEL), D_MODEL ** -0.5),
        'w_mv': nrm((DEPTH, D_MODEL, D_MODEL), D_MODEL ** -0.5),
        'w_mo': nrm((DEPTH, D_MODEL, D_MODEL), D_MODEL ** -0.5),
        'g_mem_post': gain(D_MODEL),
        'g_ffn_pre': gain(D_MODEL),
        'w_up': nrm((DEPTH, D_MODEL, 2 * D_FF), D_MODEL ** -0.5),
        'ffn_conv_w': nrm((DEPTH, FFN_K, 2 * D_FF), FFN_K ** -0.5),
        'w_down': nrm((DEPTH, D_FF, D_MODEL), D_FF ** -0.5),
        'g_ffn_post': gain(D_MODEL),
    }


def reference(x_prompt, x_sample, mem_prompt, cache_kv_latent, cache_k_rope, cache_mem_k, cache_mem_v,
              state_sconv, state_conformer, state_ffn, page_table,
              g_mix_pre, w_in, g_q, w_uq, g_kv, w_uk, w_uv, w_mla_o, sc_conv_w, w_sc_o,
              cf_conv_w, cf_conv_b, cf_ln_g, cf_ln_b, w_cf_o, w_out, g_mix_post,
              g_mem_pre, g_mem_src, w_mq, w_mk, w_mv, w_mo, g_mem_post,
              g_ffn_pre, w_up, ffn_conv_w, w_down, g_ffn_post):
    B, S, _ = x_prompt.shape
    Bd, T, _ = x_sample.shape
    past_len = page_table.shape[1] * PAGE_SIZE
    pos_prompt = jnp.arange(S)
    pos_sample = past_len + jnp.arange(T)
    prompt_attend = _mla_prompt_attend

    xp, xs = x_prompt, x_sample
    c_p, kr_p, sc_p, cf_p, ffn_p, mk_p, mv_p = [], [], [], [], [], [], []
    c_s, kr_s, sc_s, cf_s, ffn_s = [], [], [], [], []
    for l in range(DEPTH):
        lw = dict(g_mix_pre=g_mix_pre[l], w_in=w_in[l], g_q=g_q[l], w_uq=w_uq[l], g_kv=g_kv[l],
                  w_uk=w_uk[l], w_uv=w_uv[l], w_mla_o=w_mla_o[l], sc_conv_w=sc_conv_w[l], w_sc_o=w_sc_o[l],
                  cf_conv_w=cf_conv_w[l], cf_conv_b=cf_conv_b[l], cf_ln_g=cf_ln_g[l], cf_ln_b=cf_ln_b[l],
                  w_cf_o=w_cf_o[l], w_out=w_out[l], g_mix_post=g_mix_post[l], g_mem_pre=g_mem_pre[l],
                  w_mq=w_mq[l], w_mo=w_mo[l], g_mem_post=g_mem_post[l], g_ffn_pre=g_ffn_pre[l],
                  w_up=w_up[l], ffn_conv_w=ffn_conv_w[l], w_down=w_down[l], g_ffn_post=g_ffn_post[l])
        mk, mv = _mem_kv(mem_prompt, g_mem_src[l], w_mk[l], w_mv[l])
        xp, c, kr, scn, cfn, ffn = _layer(
            xp, mk, mv, pos_prompt,
            jnp.zeros((B, SC_K - 1, SC_WIDTH), xp.dtype),
            jnp.zeros((B, CF_K - 1, CF_WIDTH), xp.dtype),
            jnp.zeros((B, FFN_K - 1, 2 * D_FF), xp.dtype),
            lw, prompt_attend)
        c_p.append(c); kr_p.append(kr); sc_p.append(scn); cf_p.append(cfn); ffn_p.append(ffn)
        mk_p.append(mk); mv_p.append(mv)
        c_past = cache_kv_latent[l][page_table].reshape(Bd, past_len, KV_RANK)
        kr_past = cache_k_rope[l][page_table].reshape(Bd, past_len, ROPE_DIM)
        xs, c, kr, scn, cfn, ffn = _layer(
            xs, cache_mem_k[l], cache_mem_v[l], pos_sample,
            state_sconv[l], state_conformer[l], state_ffn[l],
            lw, _make_sample_attend(c_past, kr_past))
        c_s.append(c); kr_s.append(kr); sc_s.append(scn); cf_s.append(cfn); ffn_s.append(ffn)

    return (xp, xs,
            jnp.stack(c_p), jnp.stack(kr_p), jnp.stack(sc_p), jnp.stack(cf_p), jnp.stack(ffn_p),
            jnp.stack(mk_p), jnp.stack(mv_p),
            jnp.stack(c_s), jnp.stack(kr_s), jnp.stack(sc_s), jnp.stack(cf_s), jnp.stack(ffn_s))
```

```python
import functools
import math

import jax
import jax.numpy as jnp
from jax import lax
from jax.experimental import pallas as pl
from jax.experimental.pallas import tpu as pltpu

F32 = jnp.float32
BF16 = jnp.bfloat16

EPS = 1e-6
ROPE_THETA = 10000.0
LANES = 128
SUBLANES = 8
VMEM_LIMIT = 60 * 1024 * 1024
CF_HALO = 32
NEG_INF = -1e30


def _cparams(n_axes):
    return pltpu.CompilerParams(
        dimension_semantics=("arbitrary",) * n_axes,
        vmem_limit_bytes=VMEM_LIMIT)


def _const_spec(shape):
    nd = len(shape)
    return pl.BlockSpec(shape, lambda *_: (0,) * nd, pipeline_mode=pl.Buffered(1))


def _rms(x, g):
    return x * lax.rsqrt(jnp.mean(x * x, axis=-1, keepdims=True) + EPS) * g


def _dot(a, w):
    return jnp.dot(a.astype(BF16), w, preferred_element_type=F32)


def _dot_nt(a, b):
    return lax.dot_general(a, b, (((1,), (1,)), ((), ())), preferred_element_type=F32)


def _sigmoid(x):
    return 1.0 / (1.0 + jnp.exp(-x))


def _gelu_tanh(x):
    return 0.5 * x * (1.0 + jnp.tanh(math.sqrt(2.0 / math.pi) * (x + 0.044715 * (x * x * x))))


class _Cols:
    def __init__(self, q_rank, kv_rank, sc_w, cf_w, d_model):
        o = 0
        self.q = (o, o + q_rank); o += q_rank
        self.kv = (o, o + kv_rank); o += kv_rank
        self.kr = (o, o + 2 * LANES); o += 2 * LANES
        self.sc = (o, o + 3 * sc_w); o += 3 * sc_w
        self.cf = (o, o + 2 * cf_w); o += 2 * cf_w
        self.g = (o, o + 3 * d_model); o += 3 * d_model
        self.total = o


def _mla_prep(h, tab, wa_ref, gq_ref, wuq_ref, gkv_ref, wukv_ref, cols, n_heads, scale):
    hp = n_heads * LANES
    tc = tab[:, :LANES]
    ts = tab[:, LANES:]
    zq = jnp.dot(h, wa_ref[:, cols.q[0]:cols.q[1]], preferred_element_type=F32)
    qn = _rms(zq, gq_ref[...])
    q12 = _dot(qn, wuq_ref[...])
    q = jnp.concatenate(
        [(q12[:, i * LANES:(i + 1) * LANES] * tc + q12[:, hp + i * LANES:hp + (i + 1) * LANES] * ts) * scale
         for i in range(n_heads)], axis=-1)
    zkv = jnp.dot(h, wa_ref[:, cols.kv[0]:cols.kv[1]], preferred_element_type=F32)
    c = _rms(zkv, gkv_ref[...])
    zkr = jnp.dot(h, wa_ref[:, cols.kr[0]:cols.kr[1]], preferred_element_type=F32)
    kr = zkr[:, :LANES] * tc + zkr[:, LANES:] * ts
    kv = _dot(c, wukv_ref[...])
    k = jnp.concatenate([kv[:, i * LANES:(i + 1) * LANES] + kr for i in range(n_heads)], axis=-1)
    v = kv[:, hp:]
    return q, k, v, c, kr


def _conformer_tail(vc, cfb_ref, lng_ref, lnb_ref):
    y = vc + cfb_ref[...]
    mu = jnp.mean(y, axis=-1, keepdims=True)
    yc = y - mu
    var = jnp.mean(yc * yc, axis=-1, keepdims=True)
    y = yc * lax.rsqrt(var + EPS) * lng_ref[...] + lnb_ref[...]
    return y * _sigmoid(y)


def _gated_bc(h, wa_ref, cols, d_model, br_b, br_c):
    g0 = cols.g[0]
    zg0 = jnp.dot(h, wa_ref[:, g0:g0 + d_model], preferred_element_type=F32)
    zg1 = jnp.dot(h, wa_ref[:, g0 + d_model:g0 + 2 * d_model], preferred_element_type=F32)
    zg2 = jnp.dot(h, wa_ref[:, g0 + 2 * d_model:g0 + 3 * d_model], preferred_element_type=F32)
    return _sigmoid(zg0), _sigmoid(zg1) * br_b + _sigmoid(zg2) * br_c


def _stage_a_prompt_kernel(x_ref, tab_ref, gpre_ref, wa_ref, gq_ref, wuq_ref, gkv_ref, wukv_ref,
                           scw_ref, wsco_ref, cfw_ref, cfb_ref, lng_ref, lnb_ref, wcfo_ref,
                           q_ref, k_ref, v_ref, c_ref, kr_ref, g0_ref, mbc_ref, scst_ref, cfst_ref,
                           scbuf, cfbuf, *, cols, n_heads, rope_dim, nope_dim, scale, tm, sc_w, cf_w,
                           sc_k, cf_k, d_model):
    i = pl.program_id(1)

    @pl.when(i == 0)
    def _():
        scbuf[0:SUBLANES, :] = jnp.zeros((SUBLANES, sc_w), F32)
        cfbuf[0:CF_HALO, :] = jnp.zeros((CF_HALO, cf_w), F32)

    x = x_ref[0]
    h = _rms(x, gpre_ref[...]).astype(BF16)
    q, k, v, c, kr = _mla_prep(h, tab_ref[...], wa_ref, gq_ref, wuq_ref, gkv_ref, wukv_ref,
                               cols, n_heads, scale)
    q_ref[0] = q.astype(BF16)
    k_ref[0] = k.astype(BF16)
    v_ref[0] = v.astype(BF16)
    c_ref[0] = c
    kr_ref[0] = kr[:, nope_dim:nope_dim + rope_dim]

    zsc = jnp.dot(h, wa_ref[:, cols.sc[0]:cols.sc[1]], preferred_element_type=F32)
    scu = zsc[:, sc_w:2 * sc_w] * zsc[:, 2 * sc_w:]
    scbuf[SUBLANES:SUBLANES + tm, :] = scu
    uc = scw_ref[sc_k - 1:sc_k, :] * scu
    for t in range(sc_k - 1):
        off = SUBLANES - (sc_k - 1) + t
        uc = uc + scw_ref[t:t + 1, :] * scbuf[off:off + tm, :]
    br_b = _dot(zsc[:, :sc_w] * uc, wsco_ref[...])
    scst_ref[0] = scbuf[SUBLANES + tm - (sc_k - 1):SUBLANES + tm, :]
    scbuf[0:SUBLANES, :] = scbuf[tm:tm + SUBLANES, :]

    zcf = jnp.dot(h, wa_ref[:, cols.cf[0]:cols.cf[1]], preferred_element_type=F32)
    glu = zcf[:, :cf_w] * _sigmoid(zcf[:, cf_w:])
    cfbuf[CF_HALO:CF_HALO + tm, :] = glu
    vc = cfw_ref[cf_k - 1:cf_k, :] * glu
    for t in range(cf_k - 1):
        off = CF_HALO - (cf_k - 1) + t
        vc = vc + cfw_ref[t:t + 1, :] * cfbuf[off:off + tm, :]
    br_c = _dot(_conformer_tail(vc, cfb_ref, lng_ref, lnb_ref), wcfo_ref[...])
    cfst_ref[0] = cfbuf[CF_HALO + tm - (cf_k - 1):CF_HALO + tm, :]
    cfbuf[0:CF_HALO, :] = cfbuf[tm:tm + CF_HALO, :]

    g0, mbc = _gated_bc(h, wa_ref, cols, d_model, br_b, br_c)
    g0_ref[0] = g0
    mbc_ref[0] = mbc


def _stage_a_prompt(x, tab, p, dims, tm):
    b, s, d = x.shape
    cols = dims["cols"]
    nh = dims["n_heads"]
    hp = nh * LANES
    sc_w, cf_w = dims["sc_w"], dims["cf_w"]
    kern = functools.partial(
        _stage_a_prompt_kernel, cols=cols, n_heads=nh, rope_dim=dims["rope_dim"], nope_dim=dims["nope_dim"],
        scale=dims["mla_scale"], tm=tm, sc_w=sc_w, cf_w=cf_w, sc_k=dims["sc_k"], cf_k=dims["cf_k"], d_model=d)
    row = lambda w: pl.BlockSpec((1, tm, w), lambda bi, i: (bi, i, 0))
    state = lambda r, w: pl.BlockSpec((1, r, w), lambda bi, i: (bi, 0, 0))
    consts = [p["g_mix_pre"], p["wa"], p["g_q"], p["wuq"], p["g_kv"], p["wukv"], p["sc_conv_w"], p["w_sc_o"],
              p["cf_conv_w"], p["cf_conv_b"], p["cf_ln_g"], p["cf_ln_b"], p["w_cf_o"]]
    return pl.pallas_call(
        kern,
        grid=(b, s // tm),
        in_specs=[row(d), pl.BlockSpec((tm, 2 * LANES), lambda bi, i: (i, 0))] + [_const_spec(a.shape) for a in consts],
        out_specs=[row(hp), row(hp), row(nh * dims["v_dim"]), row(dims["kv_rank"]), row(dims["rope_dim"]),
                   row(d), row(d), state(dims["sc_k"] - 1, sc_w), state(dims["cf_k"] - 1, cf_w)],
        out_shape=[jax.ShapeDtypeStruct((b, s, hp), BF16), jax.ShapeDtypeStruct((b, s, hp), BF16),
                   jax.ShapeDtypeStruct((b, s, nh * dims["v_dim"]), BF16),
                   jax.ShapeDtypeStruct((b, s, dims["kv_rank"]), F32),
                   jax.ShapeDtypeStruct((b, s, dims["rope_dim"]), F32),
                   jax.ShapeDtypeStruct((b, s, d), F32), jax.ShapeDtypeStruct((b, s, d), F32),
                   jax.ShapeDtypeStruct((b, dims["sc_k"] - 1, sc_w), F32),
                   jax.ShapeDtypeStruct((b, dims["cf_k"] - 1, cf_w), F32)],
        scratch_shapes=[pltpu.VMEM((SUBLANES + tm, sc_w), F32), pltpu.VMEM((CF_HALO + tm, cf_w), F32)],
        compiler_params=_cparams(2),
        name="stage_a_prompt",
    )(x, tab, *consts)


def _stage_a_sample_kernel(x_ref, tab_ref, gpre_ref, wa_ref, gq_ref, wuq_ref, gkv_ref, wukv_ref, wukt_ref,
                           scw_ref, wsco_ref, cfw_ref, cfb_ref, lng_ref, lnb_ref, wcfo_ref, sch_ref, cfh_ref,
                           qabs_ref, qr_ref, c_ref, kr_ref, g0_ref, mbc_ref, scu_ref, glu_ref,
                           *, cols, n_heads, rope_dim, nope_dim, scale, sc_w, cf_w, sc_k, cf_k, d_model):
    x = x_ref[...]
    n = x.shape[0]
    h = _rms(x, gpre_ref[...]).astype(BF16)
    tab = jnp.broadcast_to(tab_ref[...], (n, 2 * LANES))
    q, _, _, c, kr = _mla_prep(h, tab, wa_ref, gq_ref, wuq_ref, gkv_ref, wukv_ref, cols, n_heads, scale)
    c_ref[...] = c
    kr_ref[...] = kr[:, nope_dim:nope_dim + rope_dim]
    for i in range(n_heads):
        qh = q[:, i * LANES:(i + 1) * LANES]
        qabs_ref[i] = _dot(qh, wukt_ref[i])
        qr_ref[i] = qh[:, nope_dim:nope_dim + rope_dim]

    zsc = jnp.dot(h, wa_ref[:, cols.sc[0]:cols.sc[1]], preferred_element_type=F32)
    scu = zsc[:, sc_w:2 * sc_w] * zsc[:, 2 * sc_w:]
    scu_ref[...] = scu
    uc = scw_ref[sc_k - 1:sc_k, :] * scu
    for t in range(sc_k - 1):
        uc = uc + scw_ref[t:t + 1, :] * sch_ref[t]
    br_b = _dot(zsc[:, :sc_w] * uc, wsco_ref[...])

    zcf = jnp.dot(h, wa_ref[:, cols.cf[0]:cols.cf[1]], preferred_element_type=F32)
    glu = zcf[:, :cf_w] * _sigmoid(zcf[:, cf_w:])
    glu_ref[...] = glu
    vc = cfw_ref[cf_k - 1:cf_k, :] * glu
    for t in range(cf_k - 1):
        vc = vc + cfw_ref[t:t + 1, :] * cfh_ref[t]
    br_c = _dot(_conformer_tail(vc, cfb_ref, lng_ref, lnb_ref), wcfo_ref[...])

    g0, mbc = _gated_bc(h, wa_ref, cols, d_model, br_b, br_c)
    g0_ref[...] = g0
    mbc_ref[...] = mbc


def _stage_a_sample(x, tab, p, sc_hist_t, cf_hist_t, dims):
    n, d = x.shape
    nh = dims["n_heads"]
    kern = functools.partial(
        _stage_a_sample_kernel, cols=dims["cols"], n_heads=nh, rope_dim=dims["rope_dim"],
        nope_dim=dims["nope_dim"], scale=dims["mla_scale"], sc_w=dims["sc_w"], cf_w=dims["cf_w"],
        sc_k=dims["sc_k"], cf_k=dims["cf_k"], d_model=d)
    return pl.pallas_call(
        kern,
        out_shape=[jax.ShapeDtypeStruct((nh, n, dims["kv_rank"]), F32),
                   jax.ShapeDtypeStruct((nh, n, dims["rope_dim"]), F32),
                   jax.ShapeDtypeStruct((n, dims["kv_rank"]), F32),
                   jax.ShapeDtypeStruct((n, dims["rope_dim"]), F32),
                   jax.ShapeDtypeStruct((n, d), F32), jax.ShapeDtypeStruct((n, d), F32),
                   jax.ShapeDtypeStruct((n, dims["sc_w"]), F32), jax.ShapeDtypeStruct((n, dims["cf_w"]), F32)],
        compiler_params=pltpu.CompilerParams(vmem_limit_bytes=VMEM_LIMIT),
        name="stage_a_sample",
    )(x, tab, p["g_mix_pre"], p["wa"], p["g_q"], p["wuq"], p["g_kv"], p["wukv"], p["wukt"],
      p["sc_conv_w"], p["w_sc_o"], p["cf_conv_w"], p["cf_conv_b"], p["cf_ln_g"], p["cf_ln_b"], p["w_cf_o"],
      sc_hist_t, cf_hist_t)


def _flash_kernel(q_ref, k_ref, v_ref, o_ref, m_scr, l_scr, acc_scr, *, tq, v_dim):
    qi = pl.program_id(2)
    reps = tq // LANES
    outs = []
    for hh in range(2):
        q = q_ref[0, :, hh * LANES:(hh + 1) * LANES]
        m_scr[...] = jnp.full((tq, LANES), NEG_INF, F32)
        l_scr[...] = jnp.zeros((tq, LANES), F32)
        acc_scr[...] = jnp.zeros((tq, LANES), F32)

        def step(j, masked):
            start = pl.multiple_of(j * tq, tq)
            kt = k_ref[0, pl.ds(start, tq), hh * LANES:(hh + 1) * LANES]
            vt = v_ref[0, pl.ds(start, tq), :]
            s = _dot_nt(q, kt)
            if masked:
                r = lax.broadcasted_iota(jnp.int32, (tq, tq), 0)
                cidx = lax.broadcasted_iota(jnp.int32, (tq, tq), 1)
                s = jnp.where(cidx <= r, s, NEG_INF)
            m_prev = m_scr[...]
            m_next = jnp.maximum(m_prev, jnp.max(s, axis=1, keepdims=True))
            p = jnp.exp(s - jnp.tile(m_next, (1, reps)))
            alpha = jnp.exp(m_prev - m_next)
            l_scr[...] = alpha * l_scr[...] + jnp.sum(p, axis=1, keepdims=True)
            m_scr[...] = m_next
            acc_scr[...] = alpha * acc_scr[...] + jnp.dot(p.astype(BF16), vt, preferred_element_type=F32)

        def body(j, carry):
            step(j, False)
            return carry

        lax.fori_loop(0, qi, body, 0)
        step(qi, True)
        outs.append(acc_scr[...] / l_scr[...])
    lane = lax.broadcasted_iota(jnp.int32, (tq, LANES), 1)
    o_ref[0] = jnp.where(lane < v_dim, outs[0], outs[1]).astype(o_ref.dtype)


def _flash_attention(q, k, v, dims, tq):
    b, s, hp = q.shape
    nh = dims["n_heads"]
    assert 2 * dims["v_dim"] == LANES and nh % 2 == 0
    kern = functools.partial(_flash_kernel, tq=tq, v_dim=dims["v_dim"])
    return pl.pallas_call(
        kern,
        grid=(b, nh // 2, s // tq),
        in_specs=[pl.BlockSpec((1, tq, 2 * LANES), lambda bi, hi, i: (bi, i, hi)),
                  pl.BlockSpec((1, s, 2 * LANES), lambda bi, hi, i: (bi, 0, hi)),
                  pl.BlockSpec((1, s, LANES), lambda bi, hi, i: (bi, 0, hi))],
        out_specs=pl.BlockSpec((1, tq, LANES), lambda bi, hi, i: (bi, i, hi)),
        out_shape=jax.ShapeDtypeStruct((b, s, nh * dims["v_dim"]), BF16),
        scratch_shapes=[pltpu.VMEM((tq, LANES), F32)] * 3,
        compiler_params=_cparams(3),
        name="mla_flash_attention",
    )(q, k, v)


def _decode_kernel(pt_ref, qlat_ref, qr_ref, cnew_ref, krnew_ref, ckv_hbm, ckr_hbm, o_ref,
                   cbuf, rbuf, sems, m_scr, l_scr, acc_scr, *, layer, n_seq, n_chunks, cp, page):
    total = n_seq * n_chunks

    def copies(t, slot):
        b = t // n_chunks
        j = t - b * n_chunks
        out = []
        for pi in range(cp):
            pg = pt_ref[b, j * cp + pi]
            out.append(pltpu.make_async_copy(ckv_hbm.at[layer, pg], cbuf.at[slot, pl.ds(pi * page, page)],
                                             sems.at[0, slot]))
            out.append(pltpu.make_async_copy(ckr_hbm.at[layer, pg], rbuf.at[slot, pl.ds(pi * page, page)],
                                             sems.at[1, slot]))
        return out

    for cpy in copies(0, 0):
        cpy.start()

    def body(t, carry):
        slot = t % 2
        b = t // n_chunks
        j = t - b * n_chunks

        @pl.when(t + 1 < total)
        def _():
            for cpy in copies(t + 1, 1 - slot):
                cpy.start()

        q = qlat_ref[b]
        qr = qr_ref[b]

        @pl.when(j == 0)
        def _():
            s_self = (jnp.sum(q * cnew_ref[b], axis=-1, keepdims=True)
                      + jnp.sum(qr * krnew_ref[b], axis=-1, keepdims=True))
            m_scr[...] = s_self
            l_scr[...] = jnp.ones_like(s_self)
            acc_scr[...] = jnp.broadcast_to(cnew_ref[b], acc_scr.shape)

        for cpy in copies(t, slot):
            cpy.wait()

        cb = cbuf[slot].astype(BF16)
        kb = rbuf[slot].astype(BF16)
        s = _dot_nt(q.astype(BF16), cb) + _dot_nt(qr.astype(BF16), kb)
        m_prev = m_scr[...]
        m_next = jnp.maximum(m_prev, jnp.max(s, axis=-1, keepdims=True))
        alpha = jnp.exp(m_prev - m_next)
        pr = jnp.exp(s - m_next)
        l_next = alpha * l_scr[...] + jnp.sum(pr, axis=-1, keepdims=True)
        acc = alpha * acc_scr[...] + jnp.dot(pr.astype(BF16), cb, preferred_element_type=F32)
        m_scr[...] = m_next
        l_scr[...] = l_next
        acc_scr[...] = acc

        @pl.when(j == n_chunks - 1)
        def _():
            o_ref[b] = acc / l_next

        return carry

    lax.fori_loop(0, total, body, 0)


def _decode_attention(page_table, qlat, qr, cnew, krnew, cache_kv, cache_kr, layer, cp):
    n_seq, nh, r = qlat.shape
    n_pages = page_table.shape[1]
    page = cache_kv.shape[2]
    rope = cache_kr.shape[3]
    assert n_pages % cp == 0
    kern = functools.partial(_decode_kernel, layer=layer, n_seq=n_seq, n_chunks=n_pages // cp, cp=cp, page=page)
    vm = pl.BlockSpec(memory_space=pltpu.VMEM)
    return pl.pallas_call(
        kern,
        grid_spec=pltpu.PrefetchScalarGridSpec(
            num_scalar_prefetch=1,
            grid=(1,),
            in_specs=[vm, vm, vm, vm, pl.BlockSpec(memory_space=pl.ANY), pl.BlockSpec(memory_space=pl.ANY)],
            out_specs=vm,
            scratch_shapes=[pltpu.VMEM((2, cp * page, r), F32), pltpu.VMEM((2, cp * page, rope), F32),
                            pltpu.SemaphoreType.DMA((2, 2)),
                            pltpu.VMEM((nh, 1), F32), pltpu.VMEM((nh, 1), F32), pltpu.VMEM((nh, r), F32)]),
        out_shape=jax.ShapeDtypeStruct((n_seq, nh, r), F32),
        compiler_params=_cparams(1),
        name="paged_latent_attention",
    )(page_table, qlat, qr, cnew, krnew, cache_kv, cache_kr)


def _ov_kernel(o_ref, w_ref, out_ref, *, n_heads):
    for i in range(n_heads):
        out_ref[i] = _dot(o_ref[i], w_ref[i])


def _latent_to_value(olat_t, wuv_h):
    nh, n, _ = olat_t.shape
    return pl.pallas_call(
        functools.partial(_ov_kernel, n_heads=nh),
        out_shape=jax.ShapeDtypeStruct((nh, n, wuv_h.shape[2]), F32),
        name="latent_to_value",
    )(olat_t, wuv_h)


def _stage_c_kernel(x_ref, o_ref, g0_ref, mbc_ref, wmo_ref, wout_ref, gpost_ref, y_ref):
    br_a = _dot(o_ref[...], wmo_ref[...])
    merged = g0_ref[...] * br_a + mbc_ref[...]
    mixed = _dot(merged, wout_ref[...])
    y_ref[...] = x_ref[...] + _rms(mixed, gpost_ref[...])


def _stage_c(x, o, g0, mbc, p, tm):
    n, d = x.shape
    row = lambda w: pl.BlockSpec((tm, w), lambda i: (i, 0))
    consts = [p["w_mla_o"], p["w_out"], p["g_mix_post"]]
    return pl.pallas_call(
        _stage_c_kernel,
        grid=(n // tm,),
        in_specs=[row(d), row(o.shape[1]), row(d), row(d)] + [_const_spec(a.shape) for a in consts],
        out_specs=row(d),
        out_shape=jax.ShapeDtypeStruct((n, d), F32),
        compiler_params=_cparams(1),
        name="merge_out",
    )(x, o, g0, mbc, *consts)


def _mem_kv_kernel(m_ref, g_ref, wk_ref, wv_ref, k_ref, v_ref):
    h = _rms(m_ref[...], g_ref[...]).astype(BF16)
    k_ref[...] = jnp.dot(h, wk_ref[...], preferred_element_type=F32)
    v_ref[...] = jnp.dot(h, wv_ref[...], preferred_element_type=F32)


def _mem_kv(mem, p, tm):
    n, d = mem.shape
    row = pl.BlockSpec((tm, d), lambda i: (i, 0))
    consts = [p["g_mem_src"], p["w_mk"], p["w_mv"]]
    return pl.pallas_call(
        _mem_kv_kernel,
        grid=(n // tm,),
        in_specs=[row] + [_const_spec(a.shape) for a in consts],
        out_specs=[row, row],
        out_shape=[jax.ShapeDtypeStruct((n, d), F32)] * 2,
        compiler_params=_cparams(1),
        name="mem_kv",
    )(mem, *consts)


def _mem_heads(q, k, v, n_heads, hd):
    outs = []
    for i in range(n_heads):
        sl = slice(i * hd, (i + 1) * hd)
        s = _dot_nt(q[:, sl], k[:, sl])
        e = jnp.exp(s - jnp.max(s, axis=-1, keepdims=True))
        o = jnp.dot(e.astype(BF16), v[:, sl], preferred_element_type=F32)
        outs.append(o / jnp.sum(e, axis=-1, keepdims=True))
    return jnp.concatenate(outs, axis=-1)


def _stage_d_prompt_kernel(x_ref, k_ref, v_ref, gpre_ref, wmq_ref, wmo_ref, gpost_ref, y_ref,
                           *, n_heads, hd, scale):
    x = x_ref[0]
    h = _rms(x, gpre_ref[...])
    q = (_dot(h, wmq_ref[...]) * scale).astype(BF16)
    o = _mem_heads(q, k_ref[0].astype(BF16), v_ref[0].astype(BF16), n_heads, hd)
    y_ref[0] = x + _rms(_dot(o, wmo_ref[...]), gpost_ref[...])


def _stage_d_prompt(x, mk, mv, p, dims, tm):
    b, s, d = x.shape
    nm = mk.shape[1]
    kern = functools.partial(_stage_d_prompt_kernel, n_heads=dims["mem_heads"], hd=dims["mem_hd"],
                             scale=dims["mem_scale"])
    row = pl.BlockSpec((1, tm, d), lambda bi, i: (bi, i, 0))
    kv = pl.BlockSpec((1, nm, d), lambda bi, i: (bi, 0, 0))
    consts = [p["g_mem_pre"], p["w_mq"], p["w_mo"], p["g_mem_post"]]
    return pl.pallas_call(
        kern,
        grid=(b, s // tm),
        in_specs=[row, kv, kv] + [_const_spec(a.shape) for a in consts],
        out_specs=row,
        out_shape=jax.ShapeDtypeStruct((b, s, d), F32),
        compiler_params=_cparams(2),
        name="mem_attention_prompt",
    )(x, mk, mv, *consts)


def _stage_d_sample_kernel(x_ref, k_ref, v_ref, gpre_ref, wmq_ref, wmo_ref, gpost_ref, y_ref, q_scr, o_scr,
                           *, n_heads, hd, scale, n_seq):
    b = pl.program_id(0)

    @pl.when(b == 0)
    def _():
        h = _rms(x_ref[...], gpre_ref[...])
        q_scr[...] = _dot(h, wmq_ref[...]) * scale

    q = q_scr[pl.ds(b, 1), :].astype(BF16)
    o_scr[pl.ds(b, 1), :] = _mem_heads(q, k_ref[0].astype(BF16), v_ref[0].astype(BF16), n_heads, hd)

    @pl.when(b == n_seq - 1)
    def _():
        y_ref[...] = x_ref[...] + _rms(_dot(o_scr[...], wmo_ref[...]), gpost_ref[...])


def _stage_d_sample(x, mk, mv, layer, p, dims):
    n, d = x.shape
    nm = mk.shape[1]
    kern = functools.partial(_stage_d_sample_kernel, n_heads=dims["mem_heads"], hd=dims["mem_hd"],
                             scale=dims["mem_scale"], n_seq=n)
    whole = pl.BlockSpec((n, d), lambda bi: (0, 0))
    kv = pl.BlockSpec((1, nm, d), lambda bi: (layer * n + bi, 0, 0))
    consts = [p["g_mem_pre"], p["w_mq"], p["w_mo"], p["g_mem_post"]]
    return pl.pallas_call(
        kern,
        grid=(n,),
        in_specs=[whole, kv, kv] + [_const_spec(a.shape) for a in consts],
        out_specs=whole,
        out_shape=jax.ShapeDtypeStruct((n, d), F32),
        scratch_shapes=[pltpu.VMEM((n, d), F32), pltpu.VMEM((n, d), F32)],
        compiler_params=_cparams(1),
        name="mem_attention_sample",
    )(x, mk, mv, *consts)


def _ffn_chunks(d_ff):
    n = d_ff // LANES
    for parts in (2, 1):
        if n % parts == 0:
            return parts, d_ff // parts
    return 1, d_ff


def _stage_e_prompt_kernel(x_ref, gpre_ref, wup_ref, cw_ref, wdown_ref, gpost_ref, y_ref, st_ref,
                           carry, ubuf, *, tm, d_ff, k_w, parts, ch):
    i = pl.program_id(1)

    @pl.when(i == 0)
    def _():
        carry[...] = jnp.zeros(carry.shape, F32)

    x = x_ref[0]
    h = _rms(x, gpre_ref[...]).astype(BF16)

    def conv(c0):
        u = jnp.dot(h, wup_ref[:, c0:c0 + ch], preferred_element_type=F32)
        ubuf[0:SUBLANES, :] = carry[:, c0:c0 + ch]
        ubuf[SUBLANES:SUBLANES + tm, :] = u
        uc = cw_ref[k_w - 1:k_w, c0:c0 + ch] * u
        for t in range(k_w - 1):
            off = SUBLANES - (k_w - 1) + t
            uc = uc + cw_ref[t:t + 1, c0:c0 + ch] * ubuf[off:off + tm, :]
        carry[:, c0:c0 + ch] = ubuf[tm:tm + SUBLANES, :]
        return uc

    f = None
    for pi in range(parts):
        gt = conv(pi * ch)
        val = conv(d_ff + pi * ch)
        part = _dot(_gelu_tanh(gt) * val, wdown_ref[pi * ch:(pi + 1) * ch, :])
        f = part if f is None else f + part
    y_ref[0] = x + _rms(f, gpost_ref[...])
    st_ref[0] = carry[SUBLANES - (k_w - 1):SUBLANES, :]


def _stage_e_prompt(x, p, dims, tm):
    b, s, d = x.shape
    d_ff = dims["d_ff"]
    k_w = dims["ffn_k"]
    parts, ch = _ffn_chunks(d_ff)
    kern = functools.partial(_stage_e_prompt_kernel, tm=tm, d_ff=d_ff, k_w=k_w, parts=parts, ch=ch)
    row = pl.BlockSpec((1, tm, d), lambda bi, i: (bi, i, 0))
    consts = [p["g_ffn_pre"], p["w_up"], p["ffn_conv_w"], p["w_down"], p["g_ffn_post"]]
    return pl.pallas_call(
        kern,
        grid=(b, s // tm),
        in_specs=[row] + [_const_spec(a.shape) for a in consts],
        out_specs=[row, pl.BlockSpec((1, k_w - 1, 2 * d_ff), lambda bi, i: (bi, 0, 0))],
        out_shape=[jax.ShapeDtypeStruct((b, s, d), F32), jax.ShapeDtypeStruct((b, k_w - 1, 2 * d_ff), F32)],
        scratch_shapes=[pltpu.VMEM((SUBLANES, 2 * d_ff), F32), pltpu.VMEM((SUBLANES + tm, ch), F32)],
        compiler_params=_cparams(2),
        name="conv_ffn_prompt",
    )(x, *consts)


def _stage_e_sample_kernel(x_ref, hist_ref, gpre_ref, wup_ref, cw_ref, wdown_ref, gpost_ref, y_ref, u_ref,
                           *, d_ff, k_w):
    x = x_ref[...]
    h = _rms(x, gpre_ref[...])
    u = _dot(h, wup_ref[...])
    u_ref[...] = u
    uc = cw_ref[k_w - 1:k_w, :] * u
    for t in range(k_w - 1):
        uc = uc + cw_ref[t:t + 1, :] * hist_ref[t]
    f = _dot(_gelu_tanh(uc[:, :d_ff]) * uc[:, d_ff:], wdown_ref[...])
    y_ref[...] = x + _rms(f, gpost_ref[...])


def _stage_e_sample(x, hist_t, p, dims):
    n, d = x.shape
    d_ff = dims["d_ff"]
    kern = functools.partial(_stage_e_sample_kernel, d_ff=d_ff, k_w=dims["ffn_k"])
    return pl.pallas_call(
        kern,
        out_shape=[jax.ShapeDtypeStruct((n, d), F32), jax.ShapeDtypeStruct((n, 2 * d_ff), F32)],
        compiler_params=pltpu.CompilerParams(vmem_limit_bytes=VMEM_LIMIT),
        name="conv_ffn_sample",
    )(x, hist_t, p["g_ffn_pre"], p["w_up"], p["ffn_conv_w"], p["w_down"], p["g_ffn_post"])


def _rope_table(pos, rope_dim, nope_dim):
    half = rope_dim // 2
    inv = ROPE_THETA ** (-jnp.arange(half, dtype=F32) / half)
    ang = pos.astype(F32)[:, None] * inv[None, :]
    cos, sin = jnp.cos(ang), jnp.sin(ang)
    t = pos.shape[0]
    pad = LANES - nope_dim - rope_dim
    tc = jnp.concatenate([jnp.ones((t, nope_dim), F32), cos, cos, jnp.zeros((t, pad), F32)], axis=1)
    ts = jnp.concatenate([jnp.zeros((t, nope_dim), F32), sin, sin, jnp.zeros((t, pad), F32)], axis=1)
    return jnp.concatenate([tc, ts], axis=1)


def _rot_cols(w, half):
    return jnp.concatenate([-w[..., half:], w[..., :half]], axis=-1)


def _prep_layer(l, dims, w):
    d = dims["d_model"]
    nh, nope, rope, vd = dims["n_heads"], dims["nope_dim"], dims["rope_dim"], dims["v_dim"]
    qr, kvr, sc_w, cf_w = dims["q_rank"], dims["kv_rank"], dims["sc_w"], dims["cf_w"]
    half = rope // 2
    pad = LANES - nope - rope
    w_in = w["w_in"][l]
    sizes = [qr, kvr, rope, sc_w, sc_w, sc_w, cf_w, cf_w, 3 * d]
    offs = [0]
    for sz in sizes:
        offs.append(offs[-1] + sz)
    seg = [w_in[:, offs[i]:offs[i + 1]] for i in range(len(sizes))]
    zeros = lambda n: jnp.zeros((d, n), F32)
    kr_pos = jnp.concatenate([zeros(nope), seg[2], zeros(pad)], axis=1)
    kr_rot = jnp.concatenate([zeros(nope), _rot_cols(seg[2], half), zeros(pad)], axis=1)
    wa = jnp.concatenate([seg[0], seg[1], kr_pos, kr_rot] + seg[3:], axis=1).astype(BF16)

    wuq = w["w_uq"][l].reshape(qr, nh, nope + rope)
    zq = lambda n: jnp.zeros((qr, nh, n), F32)
    uq_pos = jnp.concatenate([wuq, zq(pad)], axis=2).reshape(qr, nh * LANES)
    uq_rot = jnp.concatenate([zq(nope), _rot_cols(wuq[..., nope:], half), zq(pad)], axis=2).reshape(qr, nh * LANES)
    wuq2 = jnp.concatenate([uq_pos, uq_rot], axis=1).astype(BF16)

    wuk = w["w_uk"][l]
    wuv = w["w_uv"][l]
    uk_pad = jnp.concatenate([wuk, jnp.zeros((kvr, nh, LANES - nope), F32)], axis=2).reshape(kvr, nh * LANES)
    wukv = jnp.concatenate([uk_pad, wuv.reshape(kvr, nh * vd)], axis=1).astype(BF16)
    wukt = jnp.concatenate([wuk.transpose(1, 2, 0), jnp.zeros((nh, LANES - nope, kvr), F32)], axis=1).astype(BF16)
    wuv_h = wuv.transpose(1, 0, 2).astype(BF16)

    row = lambda name: w[name][l][None, :]
    return dict(
        wa=wa, wuq=wuq2, wukv=wukv, wukt=wukt, wuv_h=wuv_h,
        g_mix_pre=row("g_mix_pre"), g_q=row("g_q"), g_kv=row("g_kv"),
        sc_conv_w=w["sc_conv_w"][l], w_sc_o=w["w_sc_o"][l].astype(BF16),
        cf_conv_w=w["cf_conv_w"][l], cf_conv_b=row("cf_conv_b"), cf_ln_g=row("cf_ln_g"), cf_ln_b=row("cf_ln_b"),
        w_cf_o=w["w_cf_o"][l].astype(BF16), w_mla_o=w["w_mla_o"][l].astype(BF16), w_out=w["w_out"][l].astype(BF16),
        g_mix_post=row("g_mix_post"), g_mem_pre=row("g_mem_pre"), g_mem_src=row("g_mem_src"),
        w_mq=w["w_mq"][l].astype(BF16), w_mk=w["w_mk"][l].astype(BF16), w_mv=w["w_mv"][l].astype(BF16),
        w_mo=w["w_mo"][l].astype(BF16), g_mem_post=row("g_mem_post"), g_ffn_pre=row("g_ffn_pre"),
        w_up=w["w_up"][l].astype(BF16), ffn_conv_w=w["ffn_conv_w"][l], w_down=w["w_down"][l].astype(BF16),
        g_ffn_post=row("g_ffn_post"))


def _pick_tile(n, pref):
    t = min(n, pref)
    while n % t:
        t //= 2
    return t


def kernel(x_prompt, x_sample, mem_prompt, cache_kv_latent, cache_k_rope, cache_mem_k, cache_mem_v, state_sconv, state_conformer, state_ffn, page_table, g_mix_pre, w_in, g_q, w_uq, g_kv, w_uk, w_uv, w_mla_o, sc_conv_w, w_sc_o, cf_conv_w, cf_conv_b, cf_ln_g, cf_ln_b, w_cf_o, w_out, g_mix_post, g_mem_pre, g_mem_src, w_mq, w_mk, w_mv, w_mo, g_mem_post, g_ffn_pre, w_up, ffn_conv_w, w_down, g_ffn_post):
    w = dict(g_mix_pre=g_mix_pre, w_in=w_in, g_q=g_q, w_uq=w_uq, g_kv=g_kv, w_uk=w_uk, w_uv=w_uv,
             w_mla_o=w_mla_o, sc_conv_w=sc_conv_w, w_sc_o=w_sc_o, cf_conv_w=cf_conv_w, cf_conv_b=cf_conv_b,
             cf_ln_g=cf_ln_g, cf_ln_b=cf_ln_b, w_cf_o=w_cf_o, w_out=w_out, g_mix_post=g_mix_post,
             g_mem_pre=g_mem_pre, g_mem_src=g_mem_src, w_mq=w_mq, w_mk=w_mk, w_mv=w_mv, w_mo=w_mo,
             g_mem_post=g_mem_post, g_ffn_pre=g_ffn_pre, w_up=w_up, ffn_conv_w=ffn_conv_w, w_down=w_down,
             g_ffn_post=g_ffn_post)
    b, s, d = x_prompt.shape
    bd, t_dec, _ = x_sample.shape
    assert t_dec == 1, "sample group is one token per sequence"
    depth = w_in.shape[0]
    n_mem = mem_prompt.shape[1]
    _, kv_rank, n_heads, nope = w_uk.shape
    rope = cache_k_rope.shape[3]
    page = cache_kv_latent.shape[2]
    n_pages = page_table.shape[1]
    mem_heads, mem_hd = cache_mem_k.shape[3], cache_mem_k.shape[4]
    dims = dict(
        d_model=d, n_heads=n_heads, nope_dim=nope, rope_dim=rope, v_dim=w_uv.shape[3], q_rank=g_q.shape[1],
        kv_rank=kv_rank, sc_w=state_sconv.shape[3], cf_w=state_conformer.shape[3], sc_k=sc_conv_w.shape[1],
        cf_k=cf_conv_w.shape[1], ffn_k=ffn_conv_w.shape[1], d_ff=w_down.shape[1],
        mem_heads=mem_heads, mem_hd=mem_hd, mla_scale=1.0 / math.sqrt(nope + rope),
        mem_scale=1.0 / math.sqrt(mem_hd))
    dims["cols"] = _Cols(dims["q_rank"], kv_rank, dims["sc_w"], dims["cf_w"], d)
    assert nope + rope <= LANES and dims["cf_k"] - 1 <= CF_HALO and dims["sc_k"] - 1 <= SUBLANES

    tm = _pick_tile(s, 512)
    tq = _pick_tile(s, 512)
    assert tm >= CF_HALO
    cp = _pick_tile(n_pages, 8)
    past_len = n_pages * page
    tab_p = _rope_table(jnp.arange(s), rope, nope)
    tab_s = _rope_table(past_len + jnp.arange(t_dec), rope, nope)

    xp = x_prompt
    xs = x_sample.reshape(bd, d)
    mem2 = mem_prompt.reshape(b * n_mem, d)
    outs = {k: [] for k in ("c_p", "kr_p", "sc_p", "cf_p", "ffn_p", "mk_p", "mv_p",
                            "c_s", "kr_s", "sc_s", "cf_s", "ffn_s")}
    for l in range(depth):
        p = _prep_layer(l, dims, w)

        mk, mv = _mem_kv(mem2, p, _pick_tile(b * n_mem, 512))
        mk = mk.reshape(b, n_mem, d)
        mv = mv.reshape(b, n_mem, d)
        q, k, v, c, kr, g0, mbc, sc_st, cf_st = _stage_a_prompt(xp, tab_p, p, dims, tm)
        o = _flash_attention(q, k, v, dims, tq)
        x1 = _stage_c(xp.reshape(b * s, d), o.reshape(b * s, -1), g0.reshape(b * s, d), mbc.reshape(b * s, d),
                      p, _pick_tile(b * s, 512)).reshape(b, s, d)
        x2 = _stage_d_prompt(x1, mk, mv, p, dims, tm)
        xp, ffn_st = _stage_e_prompt(x2, p, dims, tm)
        outs["c_p"].append(c); outs["kr_p"].append(kr); outs["sc_p"].append(sc_st); outs["cf_p"].append(cf_st)
        outs["ffn_p"].append(ffn_st)
        outs["mk_p"].append(mk.reshape(b, n_mem, mem_heads, mem_hd))
        outs["mv_p"].append(mv.reshape(b, n_mem, mem_heads, mem_hd))

        sc_hist, cf_hist, ffn_hist = state_sconv[l], state_conformer[l], state_ffn[l]
        qabs, qr_s, c_s, kr_s, g0_s, mbc_s, scu_s, glu_s = _stage_a_sample(
            xs, tab_s, p, sc_hist.transpose(1, 0, 2), cf_hist.transpose(1, 0, 2), dims)
        olat = _decode_attention(page_table, qabs.transpose(1, 0, 2), qr_s.transpose(1, 0, 2),
                                 c_s[:, None, :], kr_s[:, None, :], cache_kv_latent, cache_k_rope, l, cp)
        o_s = _latent_to_value(olat.transpose(1, 0, 2), p["wuv_h"]).transpose(1, 0, 2).reshape(bd, -1)
        x1s = _stage_c(xs, o_s, g0_s, mbc_s, p, bd)
        x2s = _stage_d_sample(x1s, cache_mem_k.reshape(depth * bd, n_mem, d),
                              cache_mem_v.reshape(depth * bd, n_mem, d), l, p, dims)
        xs, u_s = _stage_e_sample(x2s, ffn_hist.transpose(1, 0, 2), p, dims)
        outs["c_s"].append(c_s[:, None, :]); outs["kr_s"].append(kr_s[:, None, :])
        outs["sc_s"].append(jnp.concatenate([sc_hist[:, 1:], scu_s[:, None, :]], axis=1))
        outs["cf_s"].append(jnp.concatenate([cf_hist[:, 1:], glu_s[:, None, :]], axis=1))
        outs["ffn_s"].append(jnp.concatenate([ffn_hist[:, 1:], u_s[:, None, :]], axis=1))

    st = lambda name: jnp.stack(outs[name])
    return (xp, xs.reshape(bd, t_dec, d),
            st("c_p"), st("kr_p"), st("sc_p"), st("cf_p"), st("ffn_p"), st("mk_p"), st("mv_p"),
            st("c_s"), st("kr_s"), st("sc_s"), st("cf_s"), st("ffn_s"))
```

```python
import functools
import math

import jax
import jax.numpy as jnp
from jax import lax
from jax.experimental import pallas as pl
from jax.experimental.pallas import tpu as pltpu

F32 = jnp.float32
BF16 = jnp.bfloat16

EPS = 1e-6
ROPE_THETA = 10000.0
LANES = 128
SUBLANES = 8
VMEM_LIMIT = 60 * 1024 * 1024
CF_HALO = 32
NEG_INF = -1e30


def _cparams(n_axes):
    return pltpu.CompilerParams(
        dimension_semantics=("arbitrary",) * n_axes,
        vmem_limit_bytes=VMEM_LIMIT)


def _const_spec(shape):
    nd = len(shape)
    return pl.BlockSpec(shape, lambda *_: (0,) * nd, pipeline_mode=pl.Buffered(1))


def _rms(x, g):
    return x * lax.rsqrt(jnp.mean(x * x, axis=-1, keepdims=True) + EPS) * g


def _dot(a, w):
    return jnp.dot(a.astype(BF16), w, preferred_element_type=F32)


def _dot_nt(a, b):
    return lax.dot_general(a, b, (((1,), (1,)), ((), ())), preferred_element_type=F32)


def _sigmoid(x):
    return 1.0 / (1.0 + jnp.exp(-x))


def _gelu_tanh(x):
    return 0.5 * x * (1.0 + jnp.tanh(math.sqrt(2.0 / math.pi) * (x + 0.044715 * (x * x * x))))


class _Cols:
    def __init__(self, q_rank, kv_rank, sc_w, cf_w, d_model):
        o = 0
        self.q = (o, o + q_rank); o += q_rank
        self.kv = (o, o + kv_rank); o += kv_rank
        self.kr = (o, o + 2 * LANES); o += 2 * LANES
        self.sc = (o, o + 3 * sc_w); o += 3 * sc_w
        self.cf = (o, o + 2 * cf_w); o += 2 * cf_w
        self.g = (o, o + 3 * d_model); o += 3 * d_model
        self.total = o


def _mla_prep(h, tab, wa_ref, gq_ref, wuq_ref, gkv_ref, wukv_ref, cols, n_heads, scale):
    hp = n_heads * LANES
    tc = tab[:, :LANES]
    ts = tab[:, LANES:]
    zq = jnp.dot(h, wa_ref[:, cols.q[0]:cols.q[1]], preferred_element_type=F32)
    qn = _rms(zq, gq_ref[...])
    q12 = _dot(qn, wuq_ref[...])
    q = jnp.concatenate(
        [(q12[:, i * LANES:(i + 1) * LANES] * tc + q12[:, hp + i * LANES:hp + (i + 1) * LANES] * ts) * scale
         for i in range(n_heads)], axis=-1)
    zkv = jnp.dot(h, wa_ref[:, cols.kv[0]:cols.kv[1]], preferred_element_type=F32)
    c = _rms(zkv, gkv_ref[...])
    zkr = jnp.dot(h, wa_ref[:, cols.kr[0]:cols.kr[1]], preferred_element_type=F32)
    kr = zkr[:, :LANES] * tc + zkr[:, LANES:] * ts
    kv = _dot(c, wukv_ref[...])
    k = jnp.concatenate([kv[:, i * LANES:(i + 1) * LANES] + kr for i in range(n_heads)], axis=-1)
    v = kv[:, hp:]
    return q, k, v, c, kr


def _conformer_tail(vc, cfb_ref, lng_ref, lnb_ref):
    y = vc + cfb_ref[...]
    mu = jnp.mean(y, axis=-1, keepdims=True)
    yc = y - mu
    var = jnp.mean(yc * yc, axis=-1, keepdims=True)
    y = yc * lax.rsqrt(var + EPS) * lng_ref[...] + lnb_ref[...]
    return y * _sigmoid(y)


def _gated_bc(h, wa_ref, cols, d_model, br_b, br_c):
    g0 = cols.g[0]
    zg0 = jnp.dot(h, wa_ref[:, g0:g0 + d_model], preferred_element_type=F32)
    zg1 = jnp.dot(h, wa_ref[:, g0 + d_model:g0 + 2 * d_model], preferred_element_type=F32)
    zg2 = jnp.dot(h, wa_ref[:, g0 + 2 * d_model:g0 + 3 * d_model], preferred_element_type=F32)
    return _sigmoid(zg0), _sigmoid(zg1) * br_b + _sigmoid(zg2) * br_c


def _stage_a_prompt_kernel(x_ref, tab_ref, gpre_ref, wa_ref, gq_ref, wuq_ref, gkv_ref, wukv_ref,
                           scw_ref, wsco_ref, cfw_ref, cfb_ref, lng_ref, lnb_ref, wcfo_ref,
                           q_ref, k_ref, v_ref, c_ref, kr_ref, g0_ref, mbc_ref, scst_ref, cfst_ref,
                           scbuf, cfbuf, *, cols, n_heads, rope_dim, nope_dim, scale, tm, sc_w, cf_w,
                           sc_k, cf_k, d_model):
    i = pl.program_id(1)

    @pl.when(i == 0)
    def _():
        scbuf[0:SUBLANES, :] = jnp.zeros((SUBLANES, sc_w), F32)
        cfbuf[0:CF_HALO, :] = jnp.zeros((CF_HALO, cf_w), F32)

    x = x_ref[0]
    h = _rms(x, gpre_ref[...]).astype(BF16)
    q, k, v, c, kr = _mla_prep(h, tab_ref[...], wa_ref, gq_ref, wuq_ref, gkv_ref, wukv_ref,
                               cols, n_heads, scale)
    q_ref[0] = q.astype(BF16)
    k_ref[0] = k.astype(BF16)
    v_ref[0] = v.astype(BF16)
    c_ref[0] = c
    kr_ref[0] = kr[:, nope_dim:nope_dim + rope_dim]

    zsc = jnp.dot(h, wa_ref[:, cols.sc[0]:cols.sc[1]], preferred_element_type=F32)
    scu = zsc[:, sc_w:2 * sc_w] * zsc[:, 2 * sc_w:]
    scbuf[SUBLANES:SUBLANES + tm, :] = scu
    uc = scw_ref[sc_k - 1:sc_k, :] * scu
    for t in range(sc_k - 1):
        off = SUBLANES - (sc_k - 1) + t
        uc = uc + scw_ref[t:t + 1, :] * scbuf[off:off + tm, :]
    br_b = _dot(zsc[:, :sc_w] * uc, wsco_ref[...])
    scst_ref[0] = scbuf[SUBLANES + tm - (sc_k - 1):SUBLANES + tm, :]
    scbuf[0:SUBLANES, :] = scbuf[tm:tm + SUBLANES, :]

    zcf = jnp.dot(h, wa_ref[:, cols.cf[0]:cols.cf[1]], preferred_element_type=F32)
    glu = zcf[:, :cf_w] * _sigmoid(zcf[:, cf_w:])
    cfbuf[CF_HALO:CF_HALO + tm, :] = glu
    base = CF_HALO - (cf_k - 1)
    vc = None
    for r in range(SUBLANES):
        rows = tm if r == 0 else tm + SUBLANES
        inner = None
        for a0 in range(0, CF_HALO + SUBLANES, SUBLANES):
            t = a0 + r - base
            if 0 <= t < cf_k:
                term = cfw_ref[t:t + 1, :] * cfbuf[a0:a0 + rows, :]
                inner = term if inner is None else inner + term
        if inner is not None:
            part = inner[r:r + tm, :]
            vc = part if vc is None else vc + part
    br_c = _dot(_conformer_tail(vc, cfb_ref, lng_ref, lnb_ref), wcfo_ref[...])
    cfst_ref[0] = cfbuf[CF_HALO + tm - (cf_k - 1):CF_HALO + tm, :]
    cfbuf[0:CF_HALO, :] = cfbuf[tm:tm + CF_HALO, :]

    g0, mbc = _gated_bc(h, wa_ref, cols, d_model, br_b, br_c)
    g0_ref[0] = g0
    mbc_ref[0] = mbc


def _stage_a_prompt(x, tab, p, dims, tm):
    b, s, d = x.shape
    cols = dims["cols"]
    nh = dims["n_heads"]
    hp = nh * LANES
    sc_w, cf_w = dims["sc_w"], dims["cf_w"]
    kern = functools.partial(
        _stage_a_prompt_kernel, cols=cols, n_heads=nh, rope_dim=dims["rope_dim"], nope_dim=dims["nope_dim"],
        scale=dims["mla_scale"], tm=tm, sc_w=sc_w, cf_w=cf_w, sc_k=dims["sc_k"], cf_k=dims["cf_k"], d_model=d)
    row = lambda w: pl.BlockSpec((1, tm, w), lambda bi, i: (bi, i, 0))
    state = lambda r, w: pl.BlockSpec((1, r, w), lambda bi, i: (bi, 0, 0))
    consts = [p["g_mix_pre"], p["wa"], p["g_q"], p["wuq"], p["g_kv"], p["wukv"], p["sc_conv_w"], p["w_sc_o"],
              p["cf_conv_w"], p["cf_conv_b"], p["cf_ln_g"], p["cf_ln_b"], p["w_cf_o"]]
    return pl.pallas_call(
        kern,
        grid=(b, s // tm),
        in_specs=[row(d), pl.BlockSpec((tm, 2 * LANES), lambda bi, i: (i, 0))] + [_const_spec(a.shape) for a in consts],
        out_specs=[row(hp), row(hp), row(nh * dims["v_dim"]), row(dims["kv_rank"]), row(dims["rope_dim"]),
                   row(d), row(d), state(dims["sc_k"] - 1, sc_w), state(dims["cf_k"] - 1, cf_w)],
        out_shape=[jax.ShapeDtypeStruct((b, s, hp), BF16), jax.ShapeDtypeStruct((b, s, hp), BF16),
                   jax.ShapeDtypeStruct((b, s, nh * dims["v_dim"]), BF16),
                   jax.ShapeDtypeStruct((b, s, dims["kv_rank"]), F32),
                   jax.ShapeDtypeStruct((b, s, dims["rope_dim"]), F32),
                   jax.ShapeDtypeStruct((b, s, d), F32), jax.ShapeDtypeStruct((b, s, d), F32),
                   jax.ShapeDtypeStruct((b, dims["sc_k"] - 1, sc_w), F32),
                   jax.ShapeDtypeStruct((b, dims["cf_k"] - 1, cf_w), F32)],
        scratch_shapes=[pltpu.VMEM((SUBLANES + tm, sc_w), F32), pltpu.VMEM((CF_HALO + tm, cf_w), F32)],
        compiler_params=_cparams(2),
        name="stage_a_prompt",
    )(x, tab, *consts)


def _stage_a_sample_kernel(x_ref, tab_ref, gpre_ref, wa_ref, gq_ref, wuq_ref, gkv_ref, wukv_ref, wukt_ref,
                           scw_ref, wsco_ref, cfw_ref, cfb_ref, lng_ref, lnb_ref, wcfo_ref, sch_ref, cfh_ref,
                           qabs_ref, qr_ref, c_ref, kr_ref, g0_ref, mbc_ref, scu_ref, glu_ref,
                           *, cols, n_heads, rope_dim, nope_dim, scale, sc_w, cf_w, sc_k, cf_k, d_model):
    x = x_ref[...]
    n = x.shape[0]
    h = _rms(x, gpre_ref[...]).astype(BF16)
    tab = jnp.broadcast_to(tab_ref[...], (n, 2 * LANES))
    q, _, _, c, kr = _mla_prep(h, tab, wa_ref, gq_ref, wuq_ref, gkv_ref, wukv_ref, cols, n_heads, scale)
    c_ref[...] = c
    kr_ref[...] = kr[:, nope_dim:nope_dim + rope_dim]
    for i in range(n_heads):
        qh = q[:, i * LANES:(i + 1) * LANES]
        qabs_ref[i] = _dot(qh, wukt_ref[i])
        qr_ref[i] = qh[:, nope_dim:nope_dim + rope_dim]

    zsc = jnp.dot(h, wa_ref[:, cols.sc[0]:cols.sc[1]], preferred_element_type=F32)
    scu = zsc[:, sc_w:2 * sc_w] * zsc[:, 2 * sc_w:]
    scu_ref[...] = scu
    uc = scw_ref[sc_k - 1:sc_k, :] * scu
    for t in range(sc_k - 1):
        uc = uc + scw_ref[t:t + 1, :] * sch_ref[t]
    br_b = _dot(zsc[:, :sc_w] * uc, wsco_ref[...])

    zcf = jnp.dot(h, wa_ref[:, cols.cf[0]:cols.cf[1]], preferred_element_type=F32)
    glu = zcf[:, :cf_w] * _sigmoid(zcf[:, cf_w:])
    glu_ref[...] = glu
    vc = cfw_ref[cf_k - 1:cf_k, :] * glu
    for t in range(cf_k - 1):
        vc = vc + cfw_ref[t:t + 1, :] * cfh_ref[t]
    br_c = _dot(_conformer_tail(vc, cfb_ref, lng_ref, lnb_ref), wcfo_ref[...])

    g0, mbc = _gated_bc(h, wa_ref, cols, d_model, br_b, br_c)
    g0_ref[...] = g0
    mbc_ref[...] = mbc


def _stage_a_sample(x, tab, p, sc_hist_t, cf_hist_t, dims):
    n, d = x.shape
    nh = dims["n_heads"]
    kern = functools.partial(
        _stage_a_sample_kernel, cols=dims["cols"], n_heads=nh, rope_dim=dims["rope_dim"],
        nope_dim=dims["nope_dim"], scale=dims["mla_scale"], sc_w=dims["sc_w"], cf_w=dims["cf_w"],
        sc_k=dims["sc_k"], cf_k=dims["cf_k"], d_model=d)
    return pl.pallas_call(
        kern,
        out_shape=[jax.ShapeDtypeStruct((nh, n, dims["kv_rank"]), F32),
                   jax.ShapeDtypeStruct((nh, n, dims["rope_dim"]), F32),
                   jax.ShapeDtypeStruct((n, dims["kv_rank"]), F32),
                   jax.ShapeDtypeStruct((n, dims["rope_dim"]), F32),
                   jax.ShapeDtypeStruct((n, d), F32), jax.ShapeDtypeStruct((n, d), F32),
                   jax.ShapeDtypeStruct((n, dims["sc_w"]), F32), jax.ShapeDtypeStruct((n, dims["cf_w"]), F32)],
        compiler_params=pltpu.CompilerParams(vmem_limit_bytes=VMEM_LIMIT),
        name="stage_a_sample",
    )(x, tab, p["g_mix_pre"], p["wa"], p["g_q"], p["wuq"], p["g_kv"], p["wukv"], p["wukt"],
      p["sc_conv_w"], p["w_sc_o"], p["cf_conv_w"], p["cf_conv_b"], p["cf_ln_g"], p["cf_ln_b"], p["w_cf_o"],
      sc_hist_t, cf_hist_t)


def _flash_kernel(q_ref, k_ref, v_ref, o_ref, m_scr, l_scr, acc_scr, *, tq, v_dim):
    qi = pl.program_id(2)
    reps = tq // LANES
    outs = []
    for hh in range(2):
        q = q_ref[0, :, hh * LANES:(hh + 1) * LANES]
        m_scr[...] = jnp.full((tq, LANES), NEG_INF, F32)
        l_scr[...] = jnp.zeros((tq, LANES), F32)
        acc_scr[...] = jnp.zeros((tq, LANES), F32)

        def step(j, masked):
            start = pl.multiple_of(j * tq, tq)
            kt = k_ref[0, pl.ds(start, tq), hh * LANES:(hh + 1) * LANES]
            vt = v_ref[0, pl.ds(start, tq), :]
            s = _dot_nt(q, kt)
            if masked:
                r = lax.broadcasted_iota(jnp.int32, (tq, tq), 0)
                cidx = lax.broadcasted_iota(jnp.int32, (tq, tq), 1)
                s = jnp.where(cidx <= r, s, NEG_INF)
            m_prev = m_scr[...]
            m_next = jnp.maximum(m_prev, jnp.max(s, axis=1, keepdims=True))
            p = jnp.exp(s - jnp.tile(m_next, (1, reps)))
            alpha = jnp.exp(m_prev - m_next)
            l_scr[...] = alpha * l_scr[...] + jnp.sum(p, axis=1, keepdims=True)
            m_scr[...] = m_next
            acc_scr[...] = alpha * acc_scr[...] + jnp.dot(p.astype(BF16), vt, preferred_element_type=F32)

        def body(j, carry):
            step(j, False)
            return carry

        lax.fori_loop(0, qi, body, 0)
        step(qi, True)
        outs.append(acc_scr[...] / l_scr[...])
    lane = lax.broadcasted_iota(jnp.int32, (tq, LANES), 1)
    o_ref[0] = jnp.where(lane < v_dim, outs[0], outs[1]).astype(o_ref.dtype)


def _flash_attention(q, k, v, dims, tq):
    b, s, hp = q.shape
    nh = dims["n_heads"]
    assert 2 * dims["v_dim"] == LANES and nh % 2 == 0
    kern = functools.partial(_flash_kernel, tq=tq, v_dim=dims["v_dim"])
    return pl.pallas_call(
        kern,
        grid=(b, nh // 2, s // tq),
        in_specs=[pl.BlockSpec((1, tq, 2 * LANES), lambda bi, hi, i: (bi, i, hi)),
                  pl.BlockSpec((1, s, 2 * LANES), lambda bi, hi, i: (bi, 0, hi)),
                  pl.BlockSpec((1, s, LANES), lambda bi, hi, i: (bi, 0, hi))],
        out_specs=pl.BlockSpec((1, tq, LANES), lambda bi, hi, i: (bi, i, hi)),
        out_shape=jax.ShapeDtypeStruct((b, s, nh * dims["v_dim"]), BF16),
        scratch_shapes=[pltpu.VMEM((tq, LANES), F32)] * 3,
        compiler_params=_cparams(3),
        name="mla_flash_attention",
    )(q, k, v)


def _decode_kernel(pt_ref, qlat_ref, qr_ref, cnew_ref, krnew_ref, ckv_hbm, ckr_hbm, o_ref,
                   cbuf, rbuf, sems, *, layer, n_seq, n_pages, page):
    def page_copies(b, slot, pi):
        pg = pt_ref[b, pi]
        off = pl.multiple_of(pi * page, page)
        return (pltpu.make_async_copy(ckv_hbm.at[layer, pg], cbuf.at[slot, pl.ds(off, page)], sems.at[0, slot]),
                pltpu.make_async_copy(ckr_hbm.at[layer, pg], rbuf.at[slot, :, pl.ds(off, page)], sems.at[1, slot]))

    def start_seq(b, slot):
        def body(pi, carry):
            for cpy in page_copies(b, slot, pi):
                cpy.start()
            return carry
        lax.fori_loop(0, n_pages, body, 0)

    def wait_seq(b, slot):
        def body(pi, carry):
            for cpy in page_copies(b, slot, pi):
                cpy.wait()
            return carry
        lax.fori_loop(0, n_pages, body, 0)

    start_seq(0, 0)

    def body(b, carry):
        slot = b % 2

        @pl.when(b + 1 < n_seq)
        def _():
            start_seq(b + 1, 1 - slot)

        q = qlat_ref[b]
        qr = qr_ref[b]
        cnew = cnew_ref[b]
        s_self = (jnp.sum(q * cnew, axis=-1, keepdims=True)
                  + jnp.sum(qr * krnew_ref[b], axis=-1, keepdims=True))
        wait_seq(b, slot)
        cb = cbuf[slot].astype(BF16)
        s = _dot_nt(q.astype(BF16), cb) + jnp.dot(qr.astype(BF16), rbuf[slot].astype(BF16),
                                                    preferred_element_type=F32)
        m = jnp.maximum(jnp.max(s, axis=-1, keepdims=True), s_self)
        pr = jnp.exp(s - m)
        p_self = jnp.exp(s_self - m)
        den = jnp.sum(pr, axis=-1, keepdims=True) + p_self
        acc = jnp.dot(pr.astype(BF16), cb, preferred_element_type=F32) + p_self * cnew
        o_ref[b] = acc / den
        return carry

    lax.fori_loop(0, n_seq, body, 0)


def _decode_attention(page_table, qlat, qr, cnew, krnew, cache_kv, cache_kr_t, layer):
    n_seq, nh, r = qlat.shape
    n_pages = page_table.shape[1]
    page = cache_kv.shape[2]
    rope = cache_kr_t.shape[2]
    past = n_pages * page
    assert 2 * past * (r + rope) * 4 + past * r * 2 < VMEM_LIMIT - (8 << 20), "past does not fit in VMEM"
    kern = functools.partial(_decode_kernel, layer=layer, n_seq=n_seq, n_pages=n_pages, page=page)
    vm = pl.BlockSpec(memory_space=pltpu.VMEM)
    return pl.pallas_call(
        kern,
        grid_spec=pltpu.PrefetchScalarGridSpec(
            num_scalar_prefetch=1,
            grid=(1,),
            in_specs=[vm, vm, vm, vm, pl.BlockSpec(memory_space=pl.ANY), pl.BlockSpec(memory_space=pl.ANY)],
            out_specs=vm,
            scratch_shapes=[pltpu.VMEM((2, past, r), F32), pltpu.VMEM((2, rope, past), F32),
                            pltpu.SemaphoreType.DMA((2, 2))]),
        out_shape=jax.ShapeDtypeStruct((n_seq, nh, r), F32),
        compiler_params=_cparams(1),
        name="paged_latent_attention",
    )(page_table, qlat, qr, cnew, krnew, cache_kv, cache_kr_t)


def _ov_kernel(o_ref, w_ref, out_ref, *, n_heads):
    for i in range(n_heads):
        out_ref[i] = _dot(o_ref[i], w_ref[i])


def _latent_to_value(olat_t, wuv_h):
    nh, n, _ = olat_t.shape
    return pl.pallas_call(
        functools.partial(_ov_kernel, n_heads=nh),
        out_shape=jax.ShapeDtypeStruct((nh, n, wuv_h.shape[2]), F32),
        name="latent_to_value",
    )(olat_t, wuv_h)


def _stage_c_kernel(x_ref, o_ref, g0_ref, mbc_ref, wmo_ref, wout_ref, gpost_ref, y_ref):
    br_a = _dot(o_ref[...], wmo_ref[...])
    merged = g0_ref[...] * br_a + mbc_ref[...]
    mixed = _dot(merged, wout_ref[...])
    y_ref[...] = x_ref[...] + _rms(mixed, gpost_ref[...])


def _stage_c(x, o, g0, mbc, p, tm):
    n, d = x.shape
    row = lambda w: pl.BlockSpec((tm, w), lambda i: (i, 0))
    consts = [p["w_mla_o"], p["w_out"], p["g_mix_post"]]
    return pl.pallas_call(
        _stage_c_kernel,
        grid=(n // tm,),
        in_specs=[row(d), row(o.shape[1]), row(d), row(d)] + [_const_spec(a.shape) for a in consts],
        out_specs=row(d),
        out_shape=jax.ShapeDtypeStruct((n, d), F32),
        compiler_params=_cparams(1),
        name="merge_out",
    )(x, o, g0, mbc, *consts)


def _mem_kv_kernel(m_ref, g_ref, wk_ref, wv_ref, k_ref, v_ref):
    h = _rms(m_ref[...], g_ref[...]).astype(BF16)
    k_ref[...] = jnp.dot(h, wk_ref[...], preferred_element_type=F32)
    v_ref[...] = jnp.dot(h, wv_ref[...], preferred_element_type=F32)


def _mem_kv(mem, p, tm):
    n, d = mem.shape
    row = pl.BlockSpec((tm, d), lambda i: (i, 0))
    consts = [p["g_mem_src"], p["w_mk"], p["w_mv"]]
    return pl.pallas_call(
        _mem_kv_kernel,
        grid=(n // tm,),
        in_specs=[row] + [_const_spec(a.shape) for a in consts],
        out_specs=[row, row],
        out_shape=[jax.ShapeDtypeStruct((n, d), F32)] * 2,
        compiler_params=_cparams(1),
        name="mem_kv",
    )(mem, *consts)


def _mem_heads(q, k, v, n_heads, hd):
    outs = []
    for i in range(n_heads):
        sl = slice(i * hd, (i + 1) * hd)
        s = _dot_nt(q[:, sl], k[:, sl])
        e = jnp.exp(s - jnp.max(s, axis=-1, keepdims=True))
        o = jnp.dot(e.astype(BF16), v[:, sl], preferred_element_type=F32)
        outs.append(o / jnp.sum(e, axis=-1, keepdims=True))
    return jnp.concatenate(outs, axis=-1)


def _stage_d_prompt_kernel(x_ref, k_ref, v_ref, gpre_ref, wmq_ref, wmo_ref, gpost_ref, y_ref,
                           *, n_heads, hd, scale):
    x = x_ref[0]
    h = _rms(x, gpre_ref[...])
    q = (_dot(h, wmq_ref[...]) * scale).astype(BF16)
    o = _mem_heads(q, k_ref[0].astype(BF16), v_ref[0].astype(BF16), n_heads, hd)
    y_ref[0] = x + _rms(_dot(o, wmo_ref[...]), gpost_ref[...])


def _stage_d_prompt(x, mk, mv, p, dims, tm):
    b, s, d = x.shape
    nm = mk.shape[1]
    kern = functools.partial(_stage_d_prompt_kernel, n_heads=dims["mem_heads"], hd=dims["mem_hd"],
                             scale=dims["mem_scale"])
    row = pl.BlockSpec((1, tm, d), lambda bi, i: (bi, i, 0))
    kv = pl.BlockSpec((1, nm, d), lambda bi, i: (bi, 0, 0))
    consts = [p["g_mem_pre"], p["w_mq"], p["w_mo"], p["g_mem_post"]]
    return pl.pallas_call(
        kern,
        grid=(b, s // tm),
        in_specs=[row, kv, kv] + [_const_spec(a.shape) for a in consts],
        out_specs=row,
        out_shape=jax.ShapeDtypeStruct((b, s, d), F32),
        compiler_params=_cparams(2),
        name="mem_attention_prompt",
    )(x, mk, mv, *consts)


def _mem_q_sample_kernel(x_ref, gpre_ref, wmq_ref, q_ref, *, scale):
    q_ref[...] = _dot(_rms(x_ref[...], gpre_ref[...]), wmq_ref[...]) * scale


def _mem_q_sample(x, p, dims):
    return pl.pallas_call(
        functools.partial(_mem_q_sample_kernel, scale=dims["mem_scale"]),
        out_shape=jax.ShapeDtypeStruct(x.shape, F32),
        name="mem_query_sample",
    )(x, p["g_mem_pre"], p["w_mq"])


def _mem_attn_sample_kernel(q_ref, k_ref, v_ref, o_ref):
    s = jnp.sum(k_ref[0, 0] * q_ref[0][None], axis=-1, keepdims=True)
    e = jnp.exp(s - jnp.max(s, axis=0, keepdims=True))
    o_ref[0] = jnp.sum(e * v_ref[0, 0], axis=0) / jnp.sum(e, axis=0)


def _mem_attn_sample(q, mk, mv, layer):
    n, nh, hd = q.shape
    nm = mk.shape[2]
    qs = pl.BlockSpec((1, nh, hd), lambda bi: (bi, 0, 0))
    kv = pl.BlockSpec((1, 1, nm, nh, hd), lambda bi: (layer, bi, 0, 0, 0))
    return pl.pallas_call(
        _mem_attn_sample_kernel,
        grid=(n,),
        in_specs=[qs, kv, kv],
        out_specs=qs,
        out_shape=jax.ShapeDtypeStruct((n, nh, hd), F32),
        compiler_params=_cparams(1),
        name="mem_attention_sample",
    )(q, mk, mv)


def _ffn_chunks(d_ff):
    n = d_ff // LANES
    for parts in (2, 1):
        if n % parts == 0:
            return parts, d_ff // parts
    return 1, d_ff


def _stage_e_prompt_kernel(x_ref, gpre_ref, wup_ref, cw_ref, wdown_ref, gpost_ref, y_ref, st_ref,
                           carry, ubuf, *, tm, d_ff, k_w, parts, ch):
    i = pl.program_id(1)

    @pl.when(i == 0)
    def _():
        carry[...] = jnp.zeros(carry.shape, F32)

    x = x_ref[0]
    h = _rms(x, gpre_ref[...]).astype(BF16)

    def conv(c0):
        u = jnp.dot(h, wup_ref[:, c0:c0 + ch], preferred_element_type=F32)
        ubuf[0:SUBLANES, :] = carry[:, c0:c0 + ch]
        ubuf[SUBLANES:SUBLANES + tm, :] = u
        uc = cw_ref[k_w - 1:k_w, c0:c0 + ch] * u
        for t in range(k_w - 1):
            off = SUBLANES - (k_w - 1) + t
            uc = uc + cw_ref[t:t + 1, c0:c0 + ch] * ubuf[off:off + tm, :]
        carry[:, c0:c0 + ch] = ubuf[tm:tm + SUBLANES, :]
        return uc

    f = None
    for pi in range(parts):
        gt = conv(pi * ch)
        val = conv(d_ff + pi * ch)
        part = _dot(_gelu_tanh(gt) * val, wdown_ref[pi * ch:(pi + 1) * ch, :])
        f = part if f is None else f + part
    y_ref[0] = x + _rms(f, gpost_ref[...])
    st_ref[0] = carry[SUBLANES - (k_w - 1):SUBLANES, :]


def _stage_e_prompt(x, p, dims, tm):
    b, s, d = x.shape
    d_ff = dims["d_ff"]
    k_w = dims["ffn_k"]
    parts, ch = _ffn_chunks(d_ff)
    kern = functools.partial(_stage_e_prompt_kernel, tm=tm, d_ff=d_ff, k_w=k_w, parts=parts, ch=ch)
    row = pl.BlockSpec((1, tm, d), lambda bi, i: (bi, i, 0))
    consts = [p["g_ffn_pre"], p["w_up"], p["ffn_conv_w"], p["w_down"], p["g_ffn_post"]]
    return pl.pallas_call(
        kern,
        grid=(b, s // tm),
        in_specs=[row] + [_const_spec(a.shape) for a in consts],
        out_specs=[row, pl.BlockSpec((1, k_w - 1, 2 * d_ff), lambda bi, i: (bi, 0, 0))],
        out_shape=[jax.ShapeDtypeStruct((b, s, d), F32), jax.ShapeDtypeStruct((b, k_w - 1, 2 * d_ff), F32)],
        scratch_shapes=[pltpu.VMEM((SUBLANES, 2 * d_ff), F32), pltpu.VMEM((SUBLANES + tm, ch), F32)],
        compiler_params=_cparams(2),
        name="conv_ffn_prompt",
    )(x, *consts)


def _stage_e_sample_kernel(x_ref, o_ref, wmo_ref, gmpost_ref, hist_ref, gpre_ref, wup_ref, cw_ref, wdown_ref,
                           gpost_ref, y_ref, u_ref, *, d_ff, k_w):
    x = x_ref[...] + _rms(_dot(o_ref[...], wmo_ref[...]), gmpost_ref[...])
    h = _rms(x, gpre_ref[...])
    u = _dot(h, wup_ref[...])
    u_ref[...] = u
    uc = cw_ref[k_w - 1:k_w, :] * u
    for t in range(k_w - 1):
        uc = uc + cw_ref[t:t + 1, :] * hist_ref[t]
    f = _dot(_gelu_tanh(uc[:, :d_ff]) * uc[:, d_ff:], wdown_ref[...])
    y_ref[...] = x + _rms(f, gpost_ref[...])


def _stage_e_sample(x, o_mem, hist_t, p, dims):
    n, d = x.shape
    d_ff = dims["d_ff"]
    kern = functools.partial(_stage_e_sample_kernel, d_ff=d_ff, k_w=dims["ffn_k"])
    return pl.pallas_call(
        kern,
        out_shape=[jax.ShapeDtypeStruct((n, d), F32), jax.ShapeDtypeStruct((n, 2 * d_ff), F32)],
        compiler_params=pltpu.CompilerParams(vmem_limit_bytes=VMEM_LIMIT),
        name="conv_ffn_sample",
    )(x, o_mem, p["w_mo"], p["g_mem_post"], hist_t, p["g_ffn_pre"], p["w_up"], p["ffn_conv_w"], p["w_down"],
      p["g_ffn_post"])


def _rope_table(pos, rope_dim, nope_dim):
    half = rope_dim // 2
    inv = ROPE_THETA ** (-jnp.arange(half, dtype=F32) / half)
    ang = pos.astype(F32)[:, None] * inv[None, :]
    cos, sin = jnp.cos(ang), jnp.sin(ang)
    t = pos.shape[0]
    pad = LANES - nope_dim - rope_dim
    tc = jnp.concatenate([jnp.ones((t, nope_dim), F32), cos, cos, jnp.zeros((t, pad), F32)], axis=1)
    ts = jnp.concatenate([jnp.zeros((t, nope_dim), F32), sin, sin, jnp.zeros((t, pad), F32)], axis=1)
    return jnp.concatenate([tc, ts], axis=1)


def _rot_cols(w, half):
    return jnp.concatenate([-w[..., half:], w[..., :half]], axis=-1)


def _prep_layer(l, dims, w):
    d = dims["d_model"]
    nh, nope, rope, vd = dims["n_heads"], dims["nope_dim"], dims["rope_dim"], dims["v_dim"]
    qr, kvr, sc_w, cf_w = dims["q_rank"], dims["kv_rank"], dims["sc_w"], dims["cf_w"]
    half = rope // 2
    pad = LANES - nope - rope
    w_in = w["w_in"][l]
    sizes = [qr, kvr, rope, sc_w, sc_w, sc_w, cf_w, cf_w, 3 * d]
    offs = [0]
    for sz in sizes:
        offs.append(offs[-1] + sz)
    seg = [w_in[:, offs[i]:offs[i + 1]] for i in range(len(sizes))]
    zeros = lambda n: jnp.zeros((d, n), F32)
    kr_pos = jnp.concatenate([zeros(nope), seg[2], zeros(pad)], axis=1)
    kr_rot = jnp.concatenate([zeros(nope), _rot_cols(seg[2], half), zeros(pad)], axis=1)
    wa = jnp.concatenate([seg[0], seg[1], kr_pos, kr_rot] + seg[3:], axis=1).astype(BF16)

    wuq = w["w_uq"][l].reshape(qr, nh, nope + rope)
    zq = lambda n: jnp.zeros((qr, nh, n), F32)
    uq_pos = jnp.concatenate([wuq, zq(pad)], axis=2).reshape(qr, nh * LANES)
    uq_rot = jnp.concatenate([zq(nope), _rot_cols(wuq[..., nope:], half), zq(pad)], axis=2).reshape(qr, nh * LANES)
    wuq2 = jnp.concatenate([uq_pos, uq_rot], axis=1).astype(BF16)

    wuk = w["w_uk"][l]
    wuv = w["w_uv"][l]
    uk_pad = jnp.concatenate([wuk, jnp.zeros((kvr, nh, LANES - nope), F32)], axis=2).reshape(kvr, nh * LANES)
    wukv = jnp.concatenate([uk_pad, wuv.reshape(kvr, nh * vd)], axis=1).astype(BF16)
    wukt = jnp.concatenate([wuk.transpose(1, 2, 0), jnp.zeros((nh, LANES - nope, kvr), F32)], axis=1).astype(BF16)
    wuv_h = wuv.transpose(1, 0, 2).astype(BF16)

    row = lambda name: w[name][l][None, :]
    return dict(
        wa=wa, wuq=wuq2, wukv=wukv, wukt=wukt, wuv_h=wuv_h,
        g_mix_pre=row("g_mix_pre"), g_q=row("g_q"), g_kv=row("g_kv"),
        sc_conv_w=w["sc_conv_w"][l], w_sc_o=w["w_sc_o"][l].astype(BF16),
        cf_conv_w=w["cf_conv_w"][l], cf_conv_b=row("cf_conv_b"), cf_ln_g=row("cf_ln_g"), cf_ln_b=row("cf_ln_b"),
        w_cf_o=w["w_cf_o"][l].astype(BF16), w_mla_o=w["w_mla_o"][l].astype(BF16), w_out=w["w_out"][l].astype(BF16),
        g_mix_post=row("g_mix_post"), g_mem_pre=row("g_mem_pre"), g_mem_src=row("g_mem_src"),
        w_mq=w["w_mq"][l].astype(BF16), w_mk=w["w_mk"][l].astype(BF16), w_mv=w["w_mv"][l].astype(BF16),
        w_mo=w["w_mo"][l].astype(BF16), g_mem_post=row("g_mem_post"), g_ffn_pre=row("g_ffn_pre"),
        w_up=w["w_up"][l].astype(BF16), ffn_conv_w=w["ffn_conv_w"][l], w_down=w["w_down"][l].astype(BF16),
        g_ffn_post=row("g_ffn_post"))


def _pick_tile(n, pref):
    t = min(n, pref)
    while n % t:
        t //= 2
    return t


def kernel(x_prompt, x_sample, mem_prompt, cache_kv_latent, cache_k_rope, cache_mem_k, cache_mem_v, state_sconv, state_conformer, state_ffn, page_table, g_mix_pre, w_in, g_q, w_uq, g_kv, w_uk, w_uv, w_mla_o, sc_conv_w, w_sc_o, cf_conv_w, cf_conv_b, cf_ln_g, cf_ln_b, w_cf_o, w_out, g_mix_post, g_mem_pre, g_mem_src, w_mq, w_mk, w_mv, w_mo, g_mem_post, g_ffn_pre, w_up, ffn_conv_w, w_down, g_ffn_post):
    w = dict(g_mix_pre=g_mix_pre, w_in=w_in, g_q=g_q, w_uq=w_uq, g_kv=g_kv, w_uk=w_uk, w_uv=w_uv,
             w_mla_o=w_mla_o, sc_conv_w=sc_conv_w, w_sc_o=w_sc_o, cf_conv_w=cf_conv_w, cf_conv_b=cf_conv_b,
             cf_ln_g=cf_ln_g, cf_ln_b=cf_ln_b, w_cf_o=w_cf_o, w_out=w_out, g_mix_post=g_mix_post,
             g_mem_pre=g_mem_pre, g_mem_src=g_mem_src, w_mq=w_mq, w_mk=w_mk, w_mv=w_mv, w_mo=w_mo,
             g_mem_post=g_mem_post, g_ffn_pre=g_ffn_pre, w_up=w_up, ffn_conv_w=ffn_conv_w, w_down=w_down,
             g_ffn_post=g_ffn_post)
    b, s, d = x_prompt.shape
    bd, t_dec, _ = x_sample.shape
    assert t_dec == 1, "sample group is one token per sequence"
    depth = w_in.shape[0]
    n_mem = mem_prompt.shape[1]
    _, kv_rank, n_heads, nope = w_uk.shape
    rope = cache_k_rope.shape[3]
    page = cache_kv_latent.shape[2]
    n_pages = page_table.shape[1]
    mem_heads, mem_hd = cache_mem_k.shape[3], cache_mem_k.shape[4]
    dims = dict(
        d_model=d, n_heads=n_heads, nope_dim=nope, rope_dim=rope, v_dim=w_uv.shape[3], q_rank=g_q.shape[1],
        kv_rank=kv_rank, sc_w=state_sconv.shape[3], cf_w=state_conformer.shape[3], sc_k=sc_conv_w.shape[1],
        cf_k=cf_conv_w.shape[1], ffn_k=ffn_conv_w.shape[1], d_ff=w_down.shape[1],
        mem_heads=mem_heads, mem_hd=mem_hd, mla_scale=1.0 / math.sqrt(nope + rope),
        mem_scale=1.0 / math.sqrt(mem_hd))
    dims["cols"] = _Cols(dims["q_rank"], kv_rank, dims["sc_w"], dims["cf_w"], d)
    assert nope + rope <= LANES and dims["cf_k"] - 1 <= CF_HALO and dims["sc_k"] - 1 <= SUBLANES

    tm = _pick_tile(s, 512)
    tq = _pick_tile(s, 512)
    assert tm >= CF_HALO
    past_len = n_pages * page
    cache_kr_t = jnp.swapaxes(cache_k_rope, 2, 3)
    tab_p = _rope_table(jnp.arange(s), rope, nope)
    tab_s = _rope_table(past_len + jnp.arange(t_dec), rope, nope)

    xp = x_prompt
    xs = x_sample.reshape(bd, d)
    mem2 = mem_prompt.reshape(b * n_mem, d)
    outs = {k: [] for k in ("c_p", "kr_p", "sc_p", "cf_p", "ffn_p", "mk_p", "mv_p",
                            "c_s", "kr_s", "sc_s", "cf_s", "ffn_s")}
    for l in range(depth):
        p = _prep_layer(l, dims, w)

        mk, mv = _mem_kv(mem2, p, _pick_tile(b * n_mem, 512))
        mk = mk.reshape(b, n_mem, d)
        mv = mv.reshape(b, n_mem, d)
        q, k, v, c, kr, g0, mbc, sc_st, cf_st = _stage_a_prompt(xp, tab_p, p, dims, tm)
        o = _flash_attention(q, k, v, dims, tq)
        x1 = _stage_c(xp.reshape(b * s, d), o.reshape(b * s, -1), g0.reshape(b * s, d), mbc.reshape(b * s, d),
                      p, _pick_tile(b * s, 512)).reshape(b, s, d)
        x2 = _stage_d_prompt(x1, mk, mv, p, dims, tm)
        xp, ffn_st = _stage_e_prompt(x2, p, dims, tm)
        outs["c_p"].append(c); outs["kr_p"].append(kr); outs["sc_p"].append(sc_st); outs["cf_p"].append(cf_st)
        outs["ffn_p"].append(ffn_st)
        outs["mk_p"].append(mk.reshape(b, n_mem, mem_heads, mem_hd))
        outs["mv_p"].append(mv.reshape(b, n_mem, mem_heads, mem_hd))

        sc_hist, cf_hist, ffn_hist = state_sconv[l], state_conformer[l], state_ffn[l]
        qabs, qr_s, c_s, kr_s, g0_s, mbc_s, scu_s, glu_s = _stage_a_sample(
            xs, tab_s, p, sc_hist.transpose(1, 0, 2), cf_hist.transpose(1, 0, 2), dims)
        olat = _decode_attention(page_table, qabs.transpose(1, 0, 2), qr_s.transpose(1, 0, 2),
                                 c_s[:, None, :], kr_s[:, None, :], cache_kv_latent, cache_kr_t, l)
        o_s = _latent_to_value(olat.transpose(1, 0, 2), p["wuv_h"]).transpose(1, 0, 2).reshape(bd, -1)
        x1s = _stage_c(xs, o_s, g0_s, mbc_s, p, bd)
        q_mem = _mem_q_sample(x1s, p, dims).reshape(bd, mem_heads, mem_hd)
        o_mem = _mem_attn_sample(q_mem, cache_mem_k, cache_mem_v, l).reshape(bd, d)
        xs, u_s = _stage_e_sample(x1s, o_mem, ffn_hist.transpose(1, 0, 2), p, dims)
        outs["c_s"].append(c_s[:, None, :]); outs["kr_s"].append(kr_s[:, None, :])
        outs["sc_s"].append(jnp.concatenate([sc_hist[:, 1:], scu_s[:, None, :]], axis=1))
        outs["cf_s"].append(jnp.concatenate([cf_hist[:, 1:], glu_s[:, None, :]], axis=1))
        outs["ffn_s"].append(jnp.concatenate([ffn_hist[:, 1:], u_s[:, None, :]], axis=1))

    st = lambda name: jnp.stack(outs[name])
    return (xp, xs.reshape(bd, t_dec, d),
            st("c_p"), st("kr_p"), st("sc_p"), st("cf_p"), st("ffn_p"), st("mk_p"), st("mv_p"),
            st("c_s"), st("kr_s"), st("sc_s"), st("cf_s"), st("ffn_s"))
```

```python
import functools
import math

import jax
import jax.numpy as jnp
from jax import lax
from jax.experimental import pallas as pl
from jax.experimental.pallas import tpu as pltpu

F32 = jnp.float32
BF16 = jnp.bfloat16

EPS = 1e-6
ROPE_THETA = 10000.0
LANES = 128
SUBLANES = 8
VMEM_LIMIT = 60 * 1024 * 1024
CF_HALO = 32
NEG_INF = -1e30
LOG2_E = math.log2(math.e)
FLASH_TILE = 1024
DECODE_SPLITS = 2


def _cparams(n_axes):
    return pltpu.CompilerParams(
        dimension_semantics=("arbitrary",) * n_axes,
        vmem_limit_bytes=VMEM_LIMIT)


def _const_spec(shape):
    nd = len(shape)
    return pl.BlockSpec(shape, lambda *_: (0,) * nd, pipeline_mode=pl.Buffered(1))


def _rms(x, g):
    return x * lax.rsqrt(jnp.mean(x * x, axis=-1, keepdims=True) + EPS) * g


def _dot(a, w):
    return jnp.dot(a.astype(BF16), w, preferred_element_type=F32)


def _dot_nt(a, b):
    return lax.dot_general(a, b, (((1,), (1,)), ((), ())), preferred_element_type=F32)


def _sigmoid(x):
    return 1.0 / (1.0 + jnp.exp(-x))


def _gelu_tanh(x):
    return 0.5 * x * (1.0 + jnp.tanh(math.sqrt(2.0 / math.pi) * (x + 0.044715 * (x * x * x))))


class _Cols:
    def __init__(self, q_rank, kv_rank, sc_w, cf_w, d_model):
        o = 0
        self.q = (o, o + q_rank); o += q_rank
        self.kv = (o, o + kv_rank); o += kv_rank
        self.kr = (o, o + 2 * LANES); o += 2 * LANES
        self.sc = (o, o + 3 * sc_w); o += 3 * sc_w
        self.cf = (o, o + 2 * cf_w); o += 2 * cf_w
        self.g = (o, o + 3 * d_model); o += 3 * d_model
        self.total = o


def _mla_prep(h, tab, wa_ref, gq_ref, wuq_ref, gkv_ref, wukv_ref, cols, n_heads, scale):
    hp = n_heads * LANES
    tc = tab[:, :LANES]
    ts = tab[:, LANES:]
    zq = jnp.dot(h, wa_ref[:, cols.q[0]:cols.q[1]], preferred_element_type=F32)
    qn = _rms(zq, gq_ref[...])
    q12 = _dot(qn, wuq_ref[...])
    q = jnp.concatenate(
        [(q12[:, i * LANES:(i + 1) * LANES] * tc + q12[:, hp + i * LANES:hp + (i + 1) * LANES] * ts) * scale
         for i in range(n_heads)], axis=-1)
    zkv = jnp.dot(h, wa_ref[:, cols.kv[0]:cols.kv[1]], preferred_element_type=F32)
    c = _rms(zkv, gkv_ref[...])
    zkr = jnp.dot(h, wa_ref[:, cols.kr[0]:cols.kr[1]], preferred_element_type=F32)
    kr = zkr[:, :LANES] * tc + zkr[:, LANES:] * ts
    kv = _dot(c, wukv_ref[...])
    k = jnp.concatenate([kv[:, i * LANES:(i + 1) * LANES] + kr for i in range(n_heads)], axis=-1)
    v = kv[:, hp:]
    return q, k, v, c, kr


def _conformer_tail(vc, cfb_ref, lng_ref, lnb_ref):
    y = vc + cfb_ref[...]
    mu = jnp.mean(y, axis=-1, keepdims=True)
    yc = y - mu
    var = jnp.mean(yc * yc, axis=-1, keepdims=True)
    y = yc * lax.rsqrt(var + EPS) * lng_ref[...] + lnb_ref[...]
    return y * _sigmoid(y)


def _gated_bc(h, wa_ref, cols, d_model, br_b, br_c):
    g0 = cols.g[0]
    zg0 = jnp.dot(h, wa_ref[:, g0:g0 + d_model], preferred_element_type=F32)
    zg1 = jnp.dot(h, wa_ref[:, g0 + d_model:g0 + 2 * d_model], preferred_element_type=F32)
    zg2 = jnp.dot(h, wa_ref[:, g0 + 2 * d_model:g0 + 3 * d_model], preferred_element_type=F32)
    return _sigmoid(zg0), _sigmoid(zg1) * br_b + _sigmoid(zg2) * br_c


def _stage_a_prompt_kernel(x_ref, tab_ref, gpre_ref, wa_ref, gq_ref, wuq_ref, gkv_ref, wukv_ref,
                           scw_ref, wsco_ref, cfw_ref, cfb_ref, lng_ref, lnb_ref, wcfo_ref,
                           q_ref, k_ref, v_ref, c_ref, kr_ref, g0_ref, mbc_ref, scst_ref, cfst_ref,
                           scbuf, cfbuf, *, cols, n_heads, rope_dim, nope_dim, scale, tm, sc_w, cf_w,
                           sc_k, cf_k, d_model):
    i = pl.program_id(1)

    @pl.when(i == 0)
    def _():
        scbuf[0:SUBLANES, :] = jnp.zeros((SUBLANES, sc_w), F32)
        cfbuf[0:CF_HALO, :] = jnp.zeros((CF_HALO, cf_w), F32)

    x = x_ref[0]
    h = _rms(x, gpre_ref[...]).astype(BF16)
    q, k, v, c, kr = _mla_prep(h, tab_ref[...], wa_ref, gq_ref, wuq_ref, gkv_ref, wukv_ref,
                               cols, n_heads, scale)
    q_ref[0] = q.astype(BF16)
    k_ref[0] = k.astype(BF16)
    v_ref[0] = v.astype(BF16)
    c_ref[0] = c
    kr_ref[0] = kr[:, nope_dim:nope_dim + rope_dim]

    zsc = jnp.dot(h, wa_ref[:, cols.sc[0]:cols.sc[1]], preferred_element_type=F32)
    scu = zsc[:, sc_w:2 * sc_w] * zsc[:, 2 * sc_w:]
    scbuf[SUBLANES:SUBLANES + tm, :] = scu
    uc = scw_ref[sc_k - 1:sc_k, :] * scu
    for t in range(sc_k - 1):
        off = SUBLANES - (sc_k - 1) + t
        uc = uc + scw_ref[t:t + 1, :] * scbuf[off:off + tm, :]
    br_b = _dot(zsc[:, :sc_w] * uc, wsco_ref[...])
    scst_ref[0] = scbuf[SUBLANES + tm - (sc_k - 1):SUBLANES + tm, :]
    scbuf[0:SUBLANES, :] = scbuf[tm:tm + SUBLANES, :]

    zcf = jnp.dot(h, wa_ref[:, cols.cf[0]:cols.cf[1]], preferred_element_type=F32)
    glu = zcf[:, :cf_w] * _sigmoid(zcf[:, cf_w:])
    cfbuf[CF_HALO:CF_HALO + tm, :] = glu
    base = CF_HALO - (cf_k - 1)
    vc = None
    for r in range(SUBLANES):
        rows = tm if r == 0 else tm + SUBLANES
        inner = None
        for a0 in range(0, CF_HALO + SUBLANES, SUBLANES):
            t = a0 + r - base
            if 0 <= t < cf_k:
                term = cfw_ref[t:t + 1, :] * cfbuf[a0:a0 + rows, :]
                inner = term if inner is None else inner + term
        if inner is not None:
            part = inner[r:r + tm, :]
            vc = part if vc is None else vc + part
    br_c = _dot(_conformer_tail(vc, cfb_ref, lng_ref, lnb_ref), wcfo_ref[...])
    cfst_ref[0] = cfbuf[CF_HALO + tm - (cf_k - 1):CF_HALO + tm, :]
    cfbuf[0:CF_HALO, :] = cfbuf[tm:tm + CF_HALO, :]

    g0, mbc = _gated_bc(h, wa_ref, cols, d_model, br_b, br_c)
    g0_ref[0] = g0
    mbc_ref[0] = mbc


def _stage_a_prompt(x, tab, p, dims, tm):
    b, s, d = x.shape
    cols = dims["cols"]
    nh = dims["n_heads"]
    hp = nh * LANES
    sc_w, cf_w = dims["sc_w"], dims["cf_w"]
    kern = functools.partial(
        _stage_a_prompt_kernel, cols=cols, n_heads=nh, rope_dim=dims["rope_dim"], nope_dim=dims["nope_dim"],
        scale=dims["mla_scale"] * LOG2_E, tm=tm, sc_w=sc_w, cf_w=cf_w, sc_k=dims["sc_k"], cf_k=dims["cf_k"],
        d_model=d)
    row = lambda w: pl.BlockSpec((1, tm, w), lambda bi, i: (bi, i, 0))
    state = lambda r, w: pl.BlockSpec((1, r, w), lambda bi, i: (bi, 0, 0))
    consts = [p["g_mix_pre"], p["wa"], p["g_q"], p["wuq"], p["g_kv"], p["wukv"], p["sc_conv_w"], p["w_sc_o"],
              p["cf_conv_w"], p["cf_conv_b"], p["cf_ln_g"], p["cf_ln_b"], p["w_cf_o"]]
    return pl.pallas_call(
        kern,
        grid=(b, s // tm),
        in_specs=[row(d), pl.BlockSpec((tm, 2 * LANES), lambda bi, i: (i, 0))] + [_const_spec(a.shape) for a in consts],
        out_specs=[row(hp), row(hp), row(nh * dims["v_dim"]), row(dims["kv_rank"]), row(dims["rope_dim"]),
                   row(d), row(d), state(dims["sc_k"] - 1, sc_w), state(dims["cf_k"] - 1, cf_w)],
        out_shape=[jax.ShapeDtypeStruct((b, s, hp), BF16), jax.ShapeDtypeStruct((b, s, hp), BF16),
                   jax.ShapeDtypeStruct((b, s, nh * dims["v_dim"]), BF16),
                   jax.ShapeDtypeStruct((b, s, dims["kv_rank"]), F32),
                   jax.ShapeDtypeStruct((b, s, dims["rope_dim"]), F32),
                   jax.ShapeDtypeStruct((b, s, d), F32), jax.ShapeDtypeStruct((b, s, d), F32),
                   jax.ShapeDtypeStruct((b, dims["sc_k"] - 1, sc_w), F32),
                   jax.ShapeDtypeStruct((b, dims["cf_k"] - 1, cf_w), F32)],
        scratch_shapes=[pltpu.VMEM((SUBLANES + tm, sc_w), F32), pltpu.VMEM((CF_HALO + tm, cf_w), F32)],
        compiler_params=_cparams(2),
        name="stage_a_prompt",
    )(x, tab, *consts)


def _stage_a_sample_kernel(x_ref, tab_ref, gpre_ref, wa_ref, gq_ref, wuq_ref, gkv_ref, wukv_ref, wukt_ref,
                           scw_ref, wsco_ref, cfw_ref, cfb_ref, lng_ref, lnb_ref, wcfo_ref, sch_ref, cfh_ref,
                           qabs_ref, qr_ref, c_ref, kr_ref, g0_ref, mbc_ref, scu_ref, glu_ref,
                           *, cols, n_heads, rope_dim, nope_dim, scale, sc_w, cf_w, sc_k, cf_k, d_model):
    x = x_ref[...]
    n = x.shape[0]
    h = _rms(x, gpre_ref[...]).astype(BF16)
    tab = jnp.broadcast_to(tab_ref[...], (n, 2 * LANES))
    q, _, _, c, kr = _mla_prep(h, tab, wa_ref, gq_ref, wuq_ref, gkv_ref, wukv_ref, cols, n_heads, scale)
    c_ref[...] = c
    kr_ref[...] = kr[:, nope_dim:nope_dim + rope_dim]
    for i in range(n_heads):
        qh = q[:, i * LANES:(i + 1) * LANES]
        qabs_ref[i] = _dot(qh, wukt_ref[i])
        qr_ref[i] = qh[:, nope_dim:nope_dim + rope_dim]

    zsc = jnp.dot(h, wa_ref[:, cols.sc[0]:cols.sc[1]], preferred_element_type=F32)
    scu = zsc[:, sc_w:2 * sc_w] * zsc[:, 2 * sc_w:]
    scu_ref[...] = scu
    uc = scw_ref[sc_k - 1:sc_k, :] * scu
    for t in range(sc_k - 1):
        uc = uc + scw_ref[t:t + 1, :] * sch_ref[t]
    br_b = _dot(zsc[:, :sc_w] * uc, wsco_ref[...])

    zcf = jnp.dot(h, wa_ref[:, cols.cf[0]:cols.cf[1]], preferred_element_type=F32)
    glu = zcf[:, :cf_w] * _sigmoid(zcf[:, cf_w:])
    glu_ref[...] = glu
    vc = cfw_ref[cf_k - 1:cf_k, :] * glu
    for t in range(cf_k - 1):
        vc = vc + cfw_ref[t:t + 1, :] * cfh_ref[t]
    br_c = _dot(_conformer_tail(vc, cfb_ref, lng_ref, lnb_ref), wcfo_ref[...])

    g0, mbc = _gated_bc(h, wa_ref, cols, d_model, br_b, br_c)
    g0_ref[...] = g0
    mbc_ref[...] = mbc


def _stage_a_sample(x, tab, p, sc_hist_t, cf_hist_t, dims):
    n, d = x.shape
    nh = dims["n_heads"]
    kern = functools.partial(
        _stage_a_sample_kernel, cols=dims["cols"], n_heads=nh, rope_dim=dims["rope_dim"],
        nope_dim=dims["nope_dim"], scale=dims["mla_scale"], sc_w=dims["sc_w"], cf_w=dims["cf_w"],
        sc_k=dims["sc_k"], cf_k=dims["cf_k"], d_model=d)
    return pl.pallas_call(
        kern,
        out_shape=[jax.ShapeDtypeStruct((nh, n, dims["kv_rank"]), F32),
                   jax.ShapeDtypeStruct((nh, n, dims["rope_dim"]), F32),
                   jax.ShapeDtypeStruct((n, dims["kv_rank"]), F32),
                   jax.ShapeDtypeStruct((n, dims["rope_dim"]), F32),
                   jax.ShapeDtypeStruct((n, d), F32), jax.ShapeDtypeStruct((n, d), F32),
                   jax.ShapeDtypeStruct((n, dims["sc_w"]), F32), jax.ShapeDtypeStruct((n, dims["cf_w"]), F32)],
        compiler_params=pltpu.CompilerParams(vmem_limit_bytes=VMEM_LIMIT),
        name="stage_a_sample",
    )(x, tab, p["g_mix_pre"], p["wa"], p["g_q"], p["wuq"], p["g_kv"], p["wukv"], p["wukt"],
      p["sc_conv_w"], p["w_sc_o"], p["cf_conv_w"], p["cf_conv_b"], p["cf_ln_g"], p["cf_ln_b"], p["w_cf_o"],
      sc_hist_t, cf_hist_t)


def _flash_kernel(q_ref, k_ref, v_ref, o_ref, m_scr, l_scr, acc_scr, *, tq, v_dim):
    qi = pl.program_id(2)
    reps = tq // LANES
    m_scr[...] = jnp.full(m_scr.shape, NEG_INF, F32)
    l_scr[...] = jnp.zeros(l_scr.shape, F32)
    acc_scr[...] = jnp.zeros(acc_scr.shape, F32)

    def step(j, masked):
        start = pl.multiple_of(j * tq, tq)
        vt = v_ref[0, pl.ds(start, tq), :]
        for hh in range(2):
            q = q_ref[0, :, hh * LANES:(hh + 1) * LANES]
            kt = k_ref[0, pl.ds(start, tq), hh * LANES:(hh + 1) * LANES]
            s = _dot_nt(q, kt)
            if masked:
                r = lax.broadcasted_iota(jnp.int32, (tq, tq), 0)
                cidx = lax.broadcasted_iota(jnp.int32, (tq, tq), 1)
                s = jnp.where(cidx <= r, s, NEG_INF)
            m_prev = m_scr[hh]
            m_next = jnp.maximum(m_prev, jnp.max(s, axis=1, keepdims=True))
            p = jnp.exp2(s - jnp.tile(m_next, (1, reps)))
            alpha = jnp.exp2(m_prev - m_next)
            l_scr[hh] = alpha * l_scr[hh] + jnp.sum(p, axis=1, keepdims=True)
            m_scr[hh] = m_next
            acc_scr[hh] = alpha * acc_scr[hh] + jnp.dot(p.astype(BF16), vt, preferred_element_type=F32)

    def body(j, carry):
        step(j, False)
        return carry

    lax.fori_loop(0, qi, body, 0)
    step(qi, True)
    lane = lax.broadcasted_iota(jnp.int32, (tq, LANES), 1)
    o_ref[0] = jnp.where(lane < v_dim, acc_scr[0] / l_scr[0], acc_scr[1] / l_scr[1]).astype(o_ref.dtype)


def _flash_attention(q, k, v, dims, tq):
    b, s, hp = q.shape
    nh = dims["n_heads"]
    assert 2 * dims["v_dim"] == LANES and nh % 2 == 0
    kern = functools.partial(_flash_kernel, tq=tq, v_dim=dims["v_dim"])
    return pl.pallas_call(
        kern,
        grid=(b, nh // 2, s // tq),
        in_specs=[pl.BlockSpec((1, tq, 2 * LANES), lambda bi, hi, i: (bi, i, hi)),
                  pl.BlockSpec((1, s, 2 * LANES), lambda bi, hi, i: (bi, 0, hi)),
                  pl.BlockSpec((1, s, LANES), lambda bi, hi, i: (bi, 0, hi))],
        out_specs=pl.BlockSpec((1, tq, LANES), lambda bi, hi, i: (bi, i, hi)),
        out_shape=jax.ShapeDtypeStruct((b, s, nh * dims["v_dim"]), BF16),
        scratch_shapes=[pltpu.VMEM((2, tq, LANES), F32)] * 3,
        compiler_params=_cparams(3),
        name="mla_flash_attention",
    )(q, k, v)


def _decode_kernel(pt_ref, qlat_ref, qr_ref, cnew_ref, krnew_ref, ckv_hbm, ckr_hbm, o_ref,
                   cbuf, rbuf, sems, *, layer, n_seq, n_pages, page):
    def page_copies(b, slot, pi):
        pg = pt_ref[b, pi]
        off = pl.multiple_of(pi * page, page)
        return (pltpu.make_async_copy(ckv_hbm.at[layer, pg], cbuf.at[slot, pl.ds(off, page)], sems.at[0, slot]),
                pltpu.make_async_copy(ckr_hbm.at[layer, pg], rbuf.at[slot, :, pl.ds(off, page)], sems.at[1, slot]))

    def start_seq(b, slot):
        def body(pi, carry):
            for cpy in page_copies(b, slot, pi):
                cpy.start()
            return carry
        lax.fori_loop(0, n_pages, body, 0, unroll=math.gcd(n_pages, 8))

    def wait_seq(b, slot):
        def body(pi, carry):
            for cpy in page_copies(b, slot, pi):
                cpy.wait()
            return carry
        lax.fori_loop(0, n_pages, body, 0, unroll=math.gcd(n_pages, 8))

    start_seq(0, 0)

    def body(b, carry):
        slot = b % 2

        @pl.when(b + 1 < n_seq)
        def _():
            start_seq(b + 1, 1 - slot)

        q = qlat_ref[b]
        qr = qr_ref[b]
        cnew = cnew_ref[b]
        s_self = (jnp.sum(q * cnew, axis=-1, keepdims=True)
                  + jnp.sum(qr * krnew_ref[b], axis=-1, keepdims=True))
        wait_seq(b, slot)
        qb = q.astype(BF16)
        qrb = qr.astype(BF16)
        span = (n_pages * page) // DECODE_SPLITS
        m = s_self
        den = jnp.ones_like(s_self)
        acc = jnp.broadcast_to(cnew, q.shape)
        for i in range(DECODE_SPLITS):
            cb = cbuf[slot, i * span:(i + 1) * span, :].astype(BF16)
            kb = rbuf[slot, :, i * span:(i + 1) * span].astype(BF16)
            s = _dot_nt(qb, cb) + jnp.dot(qrb, kb, preferred_element_type=F32)
            m_new = jnp.maximum(m, jnp.max(s, axis=-1, keepdims=True))
            alpha = jnp.exp(m - m_new)
            pr = jnp.exp(s - m_new)
            den = alpha * den + jnp.sum(pr, axis=-1, keepdims=True)
            acc = alpha * acc + jnp.dot(pr.astype(BF16), cb, preferred_element_type=F32)
            m = m_new
        o_ref[b] = acc / den
        return carry

    lax.fori_loop(0, n_seq, body, 0)


def _decode_attention(page_table, qlat, qr, cnew, krnew, cache_kv, cache_kr_t, layer):
    n_seq, nh, r = qlat.shape
    n_pages = page_table.shape[1]
    page = cache_kv.shape[2]
    rope = cache_kr_t.shape[2]
    past = n_pages * page
    assert 2 * past * (r + rope) * 4 + past * r * 2 < VMEM_LIMIT - (8 << 20), "past does not fit in VMEM"
    kern = functools.partial(_decode_kernel, layer=layer, n_seq=n_seq, n_pages=n_pages, page=page)
    vm = pl.BlockSpec(memory_space=pltpu.VMEM)
    return pl.pallas_call(
        kern,
        grid_spec=pltpu.PrefetchScalarGridSpec(
            num_scalar_prefetch=1,
            grid=(1,),
            in_specs=[vm, vm, vm, vm, pl.BlockSpec(memory_space=pl.ANY), pl.BlockSpec(memory_space=pl.ANY)],
            out_specs=vm,
            scratch_shapes=[pltpu.VMEM((2, past, r), F32), pltpu.VMEM((2, rope, past), F32),
                            pltpu.SemaphoreType.DMA((2, 2))]),
        out_shape=jax.ShapeDtypeStruct((n_seq, nh, r), F32),
        compiler_params=_cparams(1),
        name="paged_latent_attention",
    )(page_table, qlat, qr, cnew, krnew, cache_kv, cache_kr_t)


def _ov_kernel(o_ref, w_ref, out_ref, *, n_heads):
    for i in range(n_heads):
        out_ref[i] = _dot(o_ref[i], w_ref[i])


def _latent_to_value(olat_t, wuv_h):
    nh, n, _ = olat_t.shape
    return pl.pallas_call(
        functools.partial(_ov_kernel, n_heads=nh),
        out_shape=jax.ShapeDtypeStruct((nh, n, wuv_h.shape[2]), F32),
        name="latent_to_value",
    )(olat_t, wuv_h)


def _stage_c_kernel(x_ref, o_ref, g0_ref, mbc_ref, wmo_ref, wout_ref, gpost_ref, y_ref):
    br_a = _dot(o_ref[...], wmo_ref[...])
    merged = g0_ref[...] * br_a + mbc_ref[...]
    mixed = _dot(merged, wout_ref[...])
    y_ref[...] = x_ref[...] + _rms(mixed, gpost_ref[...])


def _stage_c(x, o, g0, mbc, p, tm):
    n, d = x.shape
    row = lambda w: pl.BlockSpec((tm, w), lambda i: (i, 0))
    consts = [p["w_mla_o"], p["w_out"], p["g_mix_post"]]
    return pl.pallas_call(
        _stage_c_kernel,
        grid=(n // tm,),
        in_specs=[row(d), row(o.shape[1]), row(d), row(d)] + [_const_spec(a.shape) for a in consts],
        out_specs=row(d),
        out_shape=jax.ShapeDtypeStruct((n, d), F32),
        compiler_params=_cparams(1),
        name="merge_out",
    )(x, o, g0, mbc, *consts)


def _mem_kv_kernel(m_ref, g_ref, wk_ref, wv_ref, k_ref, v_ref):
    h = _rms(m_ref[...], g_ref[...]).astype(BF16)
    k_ref[...] = jnp.dot(h, wk_ref[...], preferred_element_type=F32)
    v_ref[...] = jnp.dot(h, wv_ref[...], preferred_element_type=F32)


def _mem_kv(mem, p, tm):
    n, d = mem.shape
    row = pl.BlockSpec((tm, d), lambda i: (i, 0))
    consts = [p["g_mem_src"], p["w_mk"], p["w_mv"]]
    return pl.pallas_call(
        _mem_kv_kernel,
        grid=(n // tm,),
        in_specs=[row] + [_const_spec(a.shape) for a in consts],
        out_specs=[row, row],
        out_shape=[jax.ShapeDtypeStruct((n, d), F32)] * 2,
        compiler_params=_cparams(1),
        name="mem_kv",
    )(mem, *consts)


def _mem_heads(q, k, v, n_heads, hd):
    outs = []
    for i in range(n_heads):
        sl = slice(i * hd, (i + 1) * hd)
        s = _dot_nt(q[:, sl], k[:, sl])
        e = jnp.exp(s - jnp.max(s, axis=-1, keepdims=True))
        o = jnp.dot(e.astype(BF16), v[:, sl], preferred_element_type=F32)
        outs.append(o / jnp.sum(e, axis=-1, keepdims=True))
    return jnp.concatenate(outs, axis=-1)


def _stage_d_prompt_kernel(x_ref, k_ref, v_ref, gpre_ref, wmq_ref, wmo_ref, gpost_ref, y_ref,
                           *, n_heads, hd, scale):
    x = x_ref[0]
    h = _rms(x, gpre_ref[...])
    q = (_dot(h, wmq_ref[...]) * scale).astype(BF16)
    o = _mem_heads(q, k_ref[0].astype(BF16), v_ref[0].astype(BF16), n_heads, hd)
    y_ref[0] = x + _rms(_dot(o, wmo_ref[...]), gpost_ref[...])


def _stage_d_prompt(x, mk, mv, p, dims, tm):
    b, s, d = x.shape
    nm = mk.shape[1]
    kern = functools.partial(_stage_d_prompt_kernel, n_heads=dims["mem_heads"], hd=dims["mem_hd"],
                             scale=dims["mem_scale"])
    row = pl.BlockSpec((1, tm, d), lambda bi, i: (bi, i, 0))
    kv = pl.BlockSpec((1, nm, d), lambda bi, i: (bi, 0, 0))
    consts = [p["g_mem_pre"], p["w_mq"], p["w_mo"], p["g_mem_post"]]
    return pl.pallas_call(
        kern,
        grid=(b, s // tm),
        in_specs=[row, kv, kv] + [_const_spec(a.shape) for a in consts],
        out_specs=row,
        out_shape=jax.ShapeDtypeStruct((b, s, d), F32),
        compiler_params=_cparams(2),
        name="mem_attention_prompt",
    )(x, mk, mv, *consts)


def _mem_q_sample_kernel(x_ref, gpre_ref, wmq_ref, q_ref, *, scale):
    q_ref[...] = _dot(_rms(x_ref[...], gpre_ref[...]), wmq_ref[...]) * scale


def _mem_q_sample(x, p, dims):
    return pl.pallas_call(
        functools.partial(_mem_q_sample_kernel, scale=dims["mem_scale"]),
        out_shape=jax.ShapeDtypeStruct(x.shape, F32),
        name="mem_query_sample",
    )(x, p["g_mem_pre"], p["w_mq"])


def _mem_attn_sample_kernel(q_ref, k_ref, v_ref, o_ref):
    s = jnp.sum(k_ref[0, 0] * q_ref[0][None], axis=-1, keepdims=True)
    e = jnp.exp(s - jnp.max(s, axis=0, keepdims=True))
    o_ref[0] = jnp.sum(e * v_ref[0, 0], axis=0) / jnp.sum(e, axis=0)


def _mem_attn_sample(q, mk, mv, layer):
    n, nh, hd = q.shape
    nm = mk.shape[2]
    qs = pl.BlockSpec((1, nh, hd), lambda bi: (bi, 0, 0))
    kv = pl.BlockSpec((1, 1, nm, nh, hd), lambda bi: (layer, bi, 0, 0, 0))
    return pl.pallas_call(
        _mem_attn_sample_kernel,
        grid=(n,),
        in_specs=[qs, kv, kv],
        out_specs=qs,
        out_shape=jax.ShapeDtypeStruct((n, nh, hd), F32),
        compiler_params=_cparams(1),
        name="mem_attention_sample",
    )(q, mk, mv)


def _ffn_chunks(d_ff):
    n = d_ff // LANES
    for parts in (2, 1):
        if n % parts == 0:
            return parts, d_ff // parts
    return 1, d_ff


def _stage_e_prompt_kernel(x_ref, gpre_ref, wup_ref, cw_ref, wdown_ref, gpost_ref, y_ref, st_ref,
                           carry, ubuf, *, tm, d_ff, k_w, parts, ch):
    i = pl.program_id(1)

    @pl.when(i == 0)
    def _():
        carry[...] = jnp.zeros(carry.shape, F32)

    x = x_ref[0]
    h = _rms(x, gpre_ref[...]).astype(BF16)

    def conv(c0):
        u = jnp.dot(h, wup_ref[:, c0:c0 + ch], preferred_element_type=F32)
        ubuf[0:SUBLANES, :] = carry[:, c0:c0 + ch]
        ubuf[SUBLANES:SUBLANES + tm, :] = u
        uc = cw_ref[k_w - 1:k_w, c0:c0 + ch] * u
        for t in range(k_w - 1):
            off = SUBLANES - (k_w - 1) + t
            uc = uc + cw_ref[t:t + 1, c0:c0 + ch] * ubuf[off:off + tm, :]
        carry[:, c0:c0 + ch] = ubuf[tm:tm + SUBLANES, :]
        return uc

    f = None
    for pi in range(parts):
        gt = conv(pi * ch)
        val = conv(d_ff + pi * ch)
        part = _dot(_gelu_tanh(gt) * val, wdown_ref[pi * ch:(pi + 1) * ch, :])
        f = part if f is None else f + part
    y_ref[0] = x + _rms(f, gpost_ref[...])
    st_ref[0] = carry[SUBLANES - (k_w - 1):SUBLANES, :]


def _stage_e_prompt(x, p, dims, tm):
    b, s, d = x.shape
    d_ff = dims["d_ff"]
    k_w = dims["ffn_k"]
    parts, ch = _ffn_chunks(d_ff)
    kern = functools.partial(_stage_e_prompt_kernel, tm=tm, d_ff=d_ff, k_w=k_w, parts=parts, ch=ch)
    row = pl.BlockSpec((1, tm, d), lambda bi, i: (bi, i, 0))
    consts = [p["g_ffn_pre"], p["w_up"], p["ffn_conv_w"], p["w_down"], p["g_ffn_post"]]
    return pl.pallas_call(
        kern,
        grid=(b, s // tm),
        in_specs=[row] + [_const_spec(a.shape) for a in consts],
        out_specs=[row, pl.BlockSpec((1, k_w - 1, 2 * d_ff), lambda bi, i: (bi, 0, 0))],
        out_shape=[jax.ShapeDtypeStruct((b, s, d), F32), jax.ShapeDtypeStruct((b, k_w - 1, 2 * d_ff), F32)],
        scratch_shapes=[pltpu.VMEM((SUBLANES, 2 * d_ff), F32), pltpu.VMEM((SUBLANES + tm, ch), F32)],
        compiler_params=_cparams(2),
        name="conv_ffn_prompt",
    )(x, *consts)


def _stage_e_sample_kernel(x_ref, o_ref, wmo_ref, gmpost_ref, hist_ref, gpre_ref, wup_ref, cw_ref, wdown_ref,
                           gpost_ref, y_ref, u_ref, *, d_ff, k_w):
    x = x_ref[...] + _rms(_dot(o_ref[...], wmo_ref[...]), gmpost_ref[...])
    h = _rms(x, gpre_ref[...])
    u = _dot(h, wup_ref[...])
    u_ref[...] = u
    uc = cw_ref[k_w - 1:k_w, :] * u
    for t in range(k_w - 1):
        uc = uc + cw_ref[t:t + 1, :] * hist_ref[t]
    f = _dot(_gelu_tanh(uc[:, :d_ff]) * uc[:, d_ff:], wdown_ref[...])
    y_ref[...] = x + _rms(f, gpost_ref[...])


def _stage_e_sample(x, o_mem, hist_t, p, dims):
    n, d = x.shape
    d_ff = dims["d_ff"]
    kern = functools.partial(_stage_e_sample_kernel, d_ff=d_ff, k_w=dims["ffn_k"])
    return pl.pallas_call(
        kern,
        out_shape=[jax.ShapeDtypeStruct((n, d), F32), jax.ShapeDtypeStruct((n, 2 * d_ff), F32)],
        compiler_params=pltpu.CompilerParams(vmem_limit_bytes=VMEM_LIMIT),
        name="conv_ffn_sample",
    )(x, o_mem, p["w_mo"], p["g_mem_post"], hist_t, p["g_ffn_pre"], p["w_up"], p["ffn_conv_w"], p["w_down"],
      p["g_ffn_post"])


def _rope_table(pos, rope_dim, nope_dim):
    half = rope_dim // 2
    inv = ROPE_THETA ** (-jnp.arange(half, dtype=F32) / half)
    ang = pos.astype(F32)[:, None] * inv[None, :]
    cos, sin = jnp.cos(ang), jnp.sin(ang)
    t = pos.shape[0]
    pad = LANES - nope_dim - rope_dim
    tc = jnp.concatenate([jnp.ones((t, nope_dim), F32), cos, cos, jnp.zeros((t, pad), F32)], axis=1)
    ts = jnp.concatenate([jnp.zeros((t, nope_dim), F32), sin, sin, jnp.zeros((t, pad), F32)], axis=1)
    return jnp.concatenate([tc, ts], axis=1)


def _rot_cols(w, half):
    return jnp.concatenate([-w[..., half:], w[..., :half]], axis=-1)


def _prep_layer(l, dims, w):
    d = dims["d_model"]
    nh, nope, rope, vd = dims["n_heads"], dims["nope_dim"], dims["rope_dim"], dims["v_dim"]
    qr, kvr, sc_w, cf_w = dims["q_rank"], dims["kv_rank"], dims["sc_w"], dims["cf_w"]
    half = rope // 2
    pad = LANES - nope - rope
    w_in = w["w_in"][l]
    sizes = [qr, kvr, rope, sc_w, sc_w, sc_w, cf_w, cf_w, 3 * d]
    offs = [0]
    for sz in sizes:
        offs.append(offs[-1] + sz)
    seg = [w_in[:, offs[i]:offs[i + 1]] for i in range(len(sizes))]
    zeros = lambda n: jnp.zeros((d, n), F32)
    kr_pos = jnp.concatenate([zeros(nope), seg[2], zeros(pad)], axis=1)
    kr_rot = jnp.concatenate([zeros(nope), _rot_cols(seg[2], half), zeros(pad)], axis=1)
    wa = jnp.concatenate([seg[0], seg[1], kr_pos, kr_rot] + seg[3:], axis=1).astype(BF16)

    wuq = w["w_uq"][l].reshape(qr, nh, nope + rope)
    zq = lambda n: jnp.zeros((qr, nh, n), F32)
    uq_pos = jnp.concatenate([wuq, zq(pad)], axis=2).reshape(qr, nh * LANES)
    uq_rot = jnp.concatenate([zq(nope), _rot_cols(wuq[..., nope:], half), zq(pad)], axis=2).reshape(qr, nh * LANES)
    wuq2 = jnp.concatenate([uq_pos, uq_rot], axis=1).astype(BF16)

    wuk = w["w_uk"][l]
    wuv = w["w_uv"][l]
    uk_pad = jnp.concatenate([wuk, jnp.zeros((kvr, nh, LANES - nope), F32)], axis=2).reshape(kvr, nh * LANES)
    wukv = jnp.concatenate([uk_pad, wuv.reshape(kvr, nh * vd)], axis=1).astype(BF16)
    wukt = jnp.concatenate([wuk.transpose(1, 2, 0), jnp.zeros((nh, LANES - nope, kvr), F32)], axis=1).astype(BF16)
    wuv_h = wuv.transpose(1, 0, 2).astype(BF16)

    row = lambda name: w[name][l][None, :]
    return dict(
        wa=wa, wuq=wuq2, wukv=wukv, wukt=wukt, wuv_h=wuv_h,
        g_mix_pre=row("g_mix_pre"), g_q=row("g_q"), g_kv=row("g_kv"),
        sc_conv_w=w["sc_conv_w"][l], w_sc_o=w["w_sc_o"][l].astype(BF16),
        cf_conv_w=w["cf_conv_w"][l], cf_conv_b=row("cf_conv_b"), cf_ln_g=row("cf_ln_g"), cf_ln_b=row("cf_ln_b"),
        w_cf_o=w["w_cf_o"][l].astype(BF16), w_mla_o=w["w_mla_o"][l].astype(BF16), w_out=w["w_out"][l].astype(BF16),
        g_mix_post=row("g_mix_post"), g_mem_pre=row("g_mem_pre"), g_mem_src=row("g_mem_src"),
        w_mq=w["w_mq"][l].astype(BF16), w_mk=w["w_mk"][l].astype(BF16), w_mv=w["w_mv"][l].astype(BF16),
        w_mo=w["w_mo"][l].astype(BF16), g_mem_post=row("g_mem_post"), g_ffn_pre=row("g_ffn_pre"),
        w_up=w["w_up"][l].astype(BF16), ffn_conv_w=w["ffn_conv_w"][l], w_down=w["w_down"][l].astype(BF16),
        g_ffn_post=row("g_ffn_post"))


def _pick_tile(n, pref):
    t = min(n, pref)
    while n % t:
        t //= 2
    return t


def kernel(x_prompt, x_sample, mem_prompt, cache_kv_latent, cache_k_rope, cache_mem_k, cache_mem_v, state_sconv, state_conformer, state_ffn, page_table, g_mix_pre, w_in, g_q, w_uq, g_kv, w_uk, w_uv, w_mla_o, sc_conv_w, w_sc_o, cf_conv_w, cf_conv_b, cf_ln_g, cf_ln_b, w_cf_o, w_out, g_mix_post, g_mem_pre, g_mem_src, w_mq, w_mk, w_mv, w_mo, g_mem_post, g_ffn_pre, w_up, ffn_conv_w, w_down, g_ffn_post):
    w = dict(g_mix_pre=g_mix_pre, w_in=w_in, g_q=g_q, w_uq=w_uq, g_kv=g_kv, w_uk=w_uk, w_uv=w_uv,
             w_mla_o=w_mla_o, sc_conv_w=sc_conv_w, w_sc_o=w_sc_o, cf_conv_w=cf_conv_w, cf_conv_b=cf_conv_b,
             cf_ln_g=cf_ln_g, cf_ln_b=cf_ln_b, w_cf_o=w_cf_o, w_out=w_out, g_mix_post=g_mix_post,
             g_mem_pre=g_mem_pre, g_mem_src=g_mem_src, w_mq=w_mq, w_mk=w_mk, w_mv=w_mv, w_mo=w_mo,
             g_mem_post=g_mem_post, g_ffn_pre=g_ffn_pre, w_up=w_up, ffn_conv_w=ffn_conv_w, w_down=w_down,
             g_ffn_post=g_ffn_post)
    b, s, d = x_prompt.shape
    bd, t_dec, _ = x_sample.shape
    assert t_dec == 1, "sample group is one token per sequence"
    depth = w_in.shape[0]
    n_mem = mem_prompt.shape[1]
    _, kv_rank, n_heads, nope = w_uk.shape
    rope = cache_k_rope.shape[3]
    page = cache_kv_latent.shape[2]
    n_pages = page_table.shape[1]
    mem_heads, mem_hd = cache_mem_k.shape[3], cache_mem_k.shape[4]
    dims = dict(
        d_model=d, n_heads=n_heads, nope_dim=nope, rope_dim=rope, v_dim=w_uv.shape[3], q_rank=g_q.shape[1],
        kv_rank=kv_rank, sc_w=state_sconv.shape[3], cf_w=state_conformer.shape[3], sc_k=sc_conv_w.shape[1],
        cf_k=cf_conv_w.shape[1], ffn_k=ffn_conv_w.shape[1], d_ff=w_down.shape[1],
        mem_heads=mem_heads, mem_hd=mem_hd, mla_scale=1.0 / math.sqrt(nope + rope),
        mem_scale=1.0 / math.sqrt(mem_hd))
    dims["cols"] = _Cols(dims["q_rank"], kv_rank, dims["sc_w"], dims["cf_w"], d)
    assert nope + rope <= LANES and dims["cf_k"] - 1 <= CF_HALO and dims["sc_k"] - 1 <= SUBLANES

    tm = _pick_tile(s, 512)
    tq = _pick_tile(s, FLASH_TILE)
    assert tm >= CF_HALO
    past_len = n_pages * page
    cache_kr_t = jnp.swapaxes(cache_k_rope, 2, 3)
    tab_p = _rope_table(jnp.arange(s), rope, nope)
    tab_s = _rope_table(past_len + jnp.arange(t_dec), rope, nope)

    xp = x_prompt
    xs = x_sample.reshape(bd, d)
    mem2 = mem_prompt.reshape(b * n_mem, d)
    outs = {k: [] for k in ("c_p", "kr_p", "sc_p", "cf_p", "ffn_p", "mk_p", "mv_p",
                            "c_s", "kr_s", "sc_s", "cf_s", "ffn_s")}
    for l in range(depth):
        p = _prep_layer(l, dims, w)

        mk, mv = _mem_kv(mem2, p, _pick_tile(b * n_mem, 512))
        mk = mk.reshape(b, n_mem, d)
        mv = mv.reshape(b, n_mem, d)
        q, k, v, c, kr, g0, mbc, sc_st, cf_st = _stage_a_prompt(xp, tab_p, p, dims, tm)
        o = _flash_attention(q, k, v, dims, tq)
        x1 = _stage_c(xp.reshape(b * s, d), o.reshape(b * s, -1), g0.reshape(b * s, d), mbc.reshape(b * s, d),
                      p, _pick_tile(b * s, 512)).reshape(b, s, d)
        x2 = _stage_d_prompt(x1, mk, mv, p, dims, tm)
        xp, ffn_st = _stage_e_prompt(x2, p, dims, tm)
        outs["c_p"].append(c); outs["kr_p"].append(kr); outs["sc_p"].append(sc_st); outs["cf_p"].append(cf_st)
        outs["ffn_p"].append(ffn_st)
        outs["mk_p"].append(mk.reshape(b, n_mem, mem_heads, mem_hd))
        outs["mv_p"].append(mv.reshape(b, n_mem, mem_heads, mem_hd))

        sc_hist, cf_hist, ffn_hist = state_sconv[l], state_conformer[l], state_ffn[l]
        qabs, qr_s, c_s, kr_s, g0_s, mbc_s, scu_s, glu_s = _stage_a_sample(
            xs, tab_s, p, sc_hist.transpose(1, 0, 2), cf_hist.transpose(1, 0, 2), dims)
        olat = _decode_attention(page_table, qabs.transpose(1, 0, 2), qr_s.transpose(1, 0, 2),
                                 c_s[:, None, :], kr_s[:, None, :], cache_kv_latent, cache_kr_t, l)
        o_s = _latent_to_value(olat.transpose(1, 0, 2), p["wuv_h"]).transpose(1, 0, 2).reshape(bd, -1)
        x1s = _stage_c(xs, o_s, g0_s, mbc_s, p, bd)
        q_mem = _mem_q_sample(x1s, p, dims).reshape(bd, mem_heads, mem_hd)
        o_mem = _mem_attn_sample(q_mem, cache_mem_k, cache_mem_v, l).reshape(bd, d)
        xs, u_s = _stage_e_sample(x1s, o_mem, ffn_hist.transpose(1, 0, 2), p, dims)
        outs["c_s"].append(c_s[:, None, :]); outs["kr_s"].append(kr_s[:, None, :])
        outs["sc_s"].append(jnp.concatenate([sc_hist[:, 1:], scu_s[:, None, :]], axis=1))
        outs["cf_s"].append(jnp.concatenate([cf_hist[:, 1:], glu_s[:, None, :]], axis=1))
        outs["ffn_s"].append(jnp.concatenate([ffn_hist[:, 1:], u_s[:, None, :]], axis=1))

    st = lambda name: jnp.stack(outs[name])
    return (xp, xs.reshape(bd, t_dec, d),
            st("c_p"), st("kr_p"), st("sc_p"), st("cf_p"), st("ffn_p"), st("mk_p"), st("mv_p"),
            st("c_s"), st("kr_s"), st("sc_s"), st("cf_s"), st("ffn_s"))
```

```python
import functools
import math

import jax
import jax.numpy as jnp
from jax import lax
from jax.experimental import pallas as pl
from jax.experimental.pallas import tpu as pltpu

F32 = jnp.float32
BF16 = jnp.bfloat16

EPS = 1e-6
ROPE_THETA = 10000.0
LANES = 128
SUBLANES = 8
VMEM_LIMIT = 60 * 1024 * 1024
CF_HALO = 32
NEG_INF = -1e30
LOG2_E = math.log2(math.e)
FLASH_TILE = 1024
DECODE_SPLITS = 2


def _cparams(n_axes):
    return pltpu.CompilerParams(
        dimension_semantics=("arbitrary",) * n_axes,
        vmem_limit_bytes=VMEM_LIMIT)


def _const_spec(shape):
    nd = len(shape)
    return pl.BlockSpec(shape, lambda *_: (0,) * nd, pipeline_mode=pl.Buffered(1))


def _rms(x, g):
    return x * lax.rsqrt(jnp.mean(x * x, axis=-1, keepdims=True) + EPS) * g


def _dot(a, w):
    return jnp.dot(a.astype(BF16), w, preferred_element_type=F32)


def _dot_nt(a, b):
    return lax.dot_general(a, b, (((1,), (1,)), ((), ())), preferred_element_type=F32)


def _sigmoid(x):
    return 1.0 / (1.0 + jnp.exp(-x))


def _gelu_tanh(x):
    return 0.5 * x * (1.0 + jnp.tanh(math.sqrt(2.0 / math.pi) * (x + 0.044715 * (x * x * x))))


class _Cols:
    def __init__(self, q_rank, kv_rank, sc_w, cf_w, d_model):
        o = 0
        self.q = (o, o + q_rank); o += q_rank
        self.kv = (o, o + kv_rank); o += kv_rank
        self.kr = (o, o + 2 * LANES); o += 2 * LANES
        self.sc = (o, o + 3 * sc_w); o += 3 * sc_w
        self.cf = (o, o + 2 * cf_w); o += 2 * cf_w
        self.g = (o, o + 3 * d_model); o += 3 * d_model
        self.total = o


def _mla_prep(h, tab, wa_ref, gq_ref, wuq_ref, gkv_ref, wukv_ref, cols, n_heads, scale):
    hp = n_heads * LANES
    tc = tab[:, :LANES]
    ts = tab[:, LANES:]
    zq = jnp.dot(h, wa_ref[:, cols.q[0]:cols.q[1]], preferred_element_type=F32)
    qn = _rms(zq, gq_ref[...])
    q12 = _dot(qn, wuq_ref[...])
    q = jnp.concatenate(
        [(q12[:, i * LANES:(i + 1) * LANES] * tc + q12[:, hp + i * LANES:hp + (i + 1) * LANES] * ts) * scale
         for i in range(n_heads)], axis=-1)
    zkv = jnp.dot(h, wa_ref[:, cols.kv[0]:cols.kv[1]], preferred_element_type=F32)
    c = _rms(zkv, gkv_ref[...])
    zkr = jnp.dot(h, wa_ref[:, cols.kr[0]:cols.kr[1]], preferred_element_type=F32)
    kr = zkr[:, :LANES] * tc + zkr[:, LANES:] * ts
    kv = _dot(c, wukv_ref[...])
    k = jnp.concatenate([kv[:, i * LANES:(i + 1) * LANES] + kr for i in range(n_heads)], axis=-1)
    v = kv[:, hp:]
    return q, k, v, c, kr


def _conformer_tail(vc, cfb_ref, lng_ref, lnb_ref):
    y = vc + cfb_ref[...]
    mu = jnp.mean(y, axis=-1, keepdims=True)
    yc = y - mu
    var = jnp.mean(yc * yc, axis=-1, keepdims=True)
    y = yc * lax.rsqrt(var + EPS) * lng_ref[...] + lnb_ref[...]
    return y * _sigmoid(y)


def _gated_bc(h, wa_ref, cols, d_model, br_b, br_c):
    g0 = cols.g[0]
    zg0 = jnp.dot(h, wa_ref[:, g0:g0 + d_model], preferred_element_type=F32)
    zg1 = jnp.dot(h, wa_ref[:, g0 + d_model:g0 + 2 * d_model], preferred_element_type=F32)
    zg2 = jnp.dot(h, wa_ref[:, g0 + 2 * d_model:g0 + 3 * d_model], preferred_element_type=F32)
    return _sigmoid(zg0), _sigmoid(zg1) * br_b + _sigmoid(zg2) * br_c


def _stage_a_prompt_kernel(x_ref, tab_ref, gpre_ref, wa_ref, gq_ref, wuq_ref, gkv_ref, wukv_ref,
                           scw_ref, wsco_ref, cfw_ref, cfb_ref, lng_ref, lnb_ref, wcfo_ref,
                           q_ref, k_ref, v_ref, c_ref, kr_ref, g0_ref, mbc_ref, scst_ref, cfst_ref,
                           scbuf, cfbuf, *, cols, n_heads, rope_dim, nope_dim, scale, tm, sc_w, cf_w,
                           sc_k, cf_k, d_model):
    i = pl.program_id(1)

    @pl.when(i == 0)
    def _():
        scbuf[0:SUBLANES, :] = jnp.zeros((SUBLANES, sc_w), F32)
        cfbuf[0:CF_HALO, :] = jnp.zeros((CF_HALO, cf_w), F32)

    x = x_ref[0]
    h = _rms(x, gpre_ref[...]).astype(BF16)
    q, k, v, c, kr = _mla_prep(h, tab_ref[...], wa_ref, gq_ref, wuq_ref, gkv_ref, wukv_ref,
                               cols, n_heads, scale)
    q_ref[0] = q.astype(BF16)
    k_ref[0] = k.astype(BF16)
    v_ref[0] = v.astype(BF16)
    c_ref[0] = c
    kr_ref[0] = kr[:, nope_dim:nope_dim + rope_dim]

    zsc = jnp.dot(h, wa_ref[:, cols.sc[0]:cols.sc[1]], preferred_element_type=F32)
    scu = zsc[:, sc_w:2 * sc_w] * zsc[:, 2 * sc_w:]
    scbuf[SUBLANES:SUBLANES + tm, :] = scu
    uc = scw_ref[sc_k - 1:sc_k, :] * scu
    for t in range(sc_k - 1):
        off = SUBLANES - (sc_k - 1) + t
        uc = uc + scw_ref[t:t + 1, :] * scbuf[off:off + tm, :]
    br_b = _dot(zsc[:, :sc_w] * uc, wsco_ref[...])
    scst_ref[0] = scbuf[SUBLANES + tm - (sc_k - 1):SUBLANES + tm, :]
    scbuf[0:SUBLANES, :] = scbuf[tm:tm + SUBLANES, :]

    zcf = jnp.dot(h, wa_ref[:, cols.cf[0]:cols.cf[1]], preferred_element_type=F32)
    glu = zcf[:, :cf_w] * _sigmoid(zcf[:, cf_w:])
    cfbuf[CF_HALO:CF_HALO + tm, :] = glu
    base = CF_HALO - (cf_k - 1)
    vc = None
    for r in range(SUBLANES):
        rows = tm if r == 0 else tm + SUBLANES
        inner = None
        for a0 in range(0, CF_HALO + SUBLANES, SUBLANES):
            t = a0 + r - base
            if 0 <= t < cf_k:
                term = cfw_ref[t:t + 1, :] * cfbuf[a0:a0 + rows, :]
                inner = term if inner is None else inner + term
        if inner is not None:
            part = inner[r:r + tm, :]
            vc = part if vc is None else vc + part
    br_c = _dot(_conformer_tail(vc, cfb_ref, lng_ref, lnb_ref), wcfo_ref[...])
    cfst_ref[0] = cfbuf[CF_HALO + tm - (cf_k - 1):CF_HALO + tm, :]
    cfbuf[0:CF_HALO, :] = cfbuf[tm:tm + CF_HALO, :]

    g0, mbc = _gated_bc(h, wa_ref, cols, d_model, br_b, br_c)
    g0_ref[0] = g0
    mbc_ref[0] = mbc


def _stage_a_prompt(x, tab, p, dims, tm):
    b, s, d = x.shape
    cols = dims["cols"]
    nh = dims["n_heads"]
    hp = nh * LANES
    sc_w, cf_w = dims["sc_w"], dims["cf_w"]
    kern = functools.partial(
        _stage_a_prompt_kernel, cols=cols, n_heads=nh, rope_dim=dims["rope_dim"], nope_dim=dims["nope_dim"],
        scale=dims["mla_scale"] * LOG2_E, tm=tm, sc_w=sc_w, cf_w=cf_w, sc_k=dims["sc_k"], cf_k=dims["cf_k"],
        d_model=d)
    row = lambda w: pl.BlockSpec((1, tm, w), lambda bi, i: (bi, i, 0))
    state = lambda r, w: pl.BlockSpec((1, r, w), lambda bi, i: (bi, 0, 0))
    consts = [p["g_mix_pre"], p["wa"], p["g_q"], p["wuq"], p["g_kv"], p["wukv"], p["sc_conv_w"], p["w_sc_o"],
              p["cf_conv_w"], p["cf_conv_b"], p["cf_ln_g"], p["cf_ln_b"], p["w_cf_o"]]
    return pl.pallas_call(
        kern,
        grid=(b, s // tm),
        in_specs=[row(d), pl.BlockSpec((tm, 2 * LANES), lambda bi, i: (i, 0))] + [_const_spec(a.shape) for a in consts],
        out_specs=[row(hp), row(hp), row(nh * dims["v_dim"]), row(dims["kv_rank"]), row(dims["rope_dim"]),
                   row(d), row(d), state(dims["sc_k"] - 1, sc_w), state(dims["cf_k"] - 1, cf_w)],
        out_shape=[jax.ShapeDtypeStruct((b, s, hp), BF16), jax.ShapeDtypeStruct((b, s, hp), BF16),
                   jax.ShapeDtypeStruct((b, s, nh * dims["v_dim"]), BF16),
                   jax.ShapeDtypeStruct((b, s, dims["kv_rank"]), F32),
                   jax.ShapeDtypeStruct((b, s, dims["rope_dim"]), F32),
                   jax.ShapeDtypeStruct((b, s, d), F32), jax.ShapeDtypeStruct((b, s, d), F32),
                   jax.ShapeDtypeStruct((b, dims["sc_k"] - 1, sc_w), F32),
                   jax.ShapeDtypeStruct((b, dims["cf_k"] - 1, cf_w), F32)],
        scratch_shapes=[pltpu.VMEM((SUBLANES + tm, sc_w), F32), pltpu.VMEM((CF_HALO + tm, cf_w), F32)],
        compiler_params=_cparams(2),
        name="stage_a_prompt",
    )(x, tab, *consts)


def _stage_a_sample_kernel(x_ref, tab_ref, gpre_ref, wa_ref, gq_ref, wuq_ref, gkv_ref, wukv_ref, wukt_ref,
                           scw_ref, wsco_ref, cfw_ref, cfb_ref, lng_ref, lnb_ref, wcfo_ref, sch_ref, cfh_ref,
                           qabs_ref, qr_ref, c_ref, kr_ref, g0_ref, mbc_ref, scu_ref, glu_ref,
                           *, cols, n_heads, rope_dim, nope_dim, scale, sc_w, cf_w, sc_k, cf_k, d_model):
    x = x_ref[...]
    n = x.shape[0]
    h = _rms(x, gpre_ref[...]).astype(BF16)
    tab = jnp.broadcast_to(tab_ref[...], (n, 2 * LANES))
    q, _, _, c, kr = _mla_prep(h, tab, wa_ref, gq_ref, wuq_ref, gkv_ref, wukv_ref, cols, n_heads, scale)
    c_ref[...] = c
    kr_ref[...] = kr[:, nope_dim:nope_dim + rope_dim]
    for i in range(n_heads):
        qh = q[:, i * LANES:(i + 1) * LANES]
        qabs_ref[i] = _dot(qh, wukt_ref[i])
        qr_ref[i] = qh[:, nope_dim:nope_dim + rope_dim]

    zsc = jnp.dot(h, wa_ref[:, cols.sc[0]:cols.sc[1]], preferred_element_type=F32)
    scu = zsc[:, sc_w:2 * sc_w] * zsc[:, 2 * sc_w:]
    scu_ref[...] = scu
    uc = scw_ref[sc_k - 1:sc_k, :] * scu
    for t in range(sc_k - 1):
        uc = uc + scw_ref[t:t + 1, :] * sch_ref[t]
    br_b = _dot(zsc[:, :sc_w] * uc, wsco_ref[...])

    zcf = jnp.dot(h, wa_ref[:, cols.cf[0]:cols.cf[1]], preferred_element_type=F32)
    glu = zcf[:, :cf_w] * _sigmoid(zcf[:, cf_w:])
    glu_ref[...] = glu
    vc = cfw_ref[cf_k - 1:cf_k, :] * glu
    for t in range(cf_k - 1):
        vc = vc + cfw_ref[t:t + 1, :] * cfh_ref[t]
    br_c = _dot(_conformer_tail(vc, cfb_ref, lng_ref, lnb_ref), wcfo_ref[...])

    g0, mbc = _gated_bc(h, wa_ref, cols, d_model, br_b, br_c)
    g0_ref[...] = g0
    mbc_ref[...] = mbc


def _stage_a_sample(x, tab, p, sc_hist_t, cf_hist_t, dims):
    n, d = x.shape
    nh = dims["n_heads"]
    kern = functools.partial(
        _stage_a_sample_kernel, cols=dims["cols"], n_heads=nh, rope_dim=dims["rope_dim"],
        nope_dim=dims["nope_dim"], scale=dims["mla_scale"], sc_w=dims["sc_w"], cf_w=dims["cf_w"],
        sc_k=dims["sc_k"], cf_k=dims["cf_k"], d_model=d)
    return pl.pallas_call(
        kern,
        out_shape=[jax.ShapeDtypeStruct((nh, n, dims["kv_rank"]), F32),
                   jax.ShapeDtypeStruct((nh, n, dims["rope_dim"]), F32),
                   jax.ShapeDtypeStruct((n, dims["kv_rank"]), F32),
                   jax.ShapeDtypeStruct((n, dims["rope_dim"]), F32),
                   jax.ShapeDtypeStruct((n, d), F32), jax.ShapeDtypeStruct((n, d), F32),
                   jax.ShapeDtypeStruct((n, dims["sc_w"]), F32), jax.ShapeDtypeStruct((n, dims["cf_w"]), F32)],
        compiler_params=pltpu.CompilerParams(vmem_limit_bytes=VMEM_LIMIT),
        name="stage_a_sample",
    )(x, tab, p["g_mix_pre"], p["wa"], p["g_q"], p["wuq"], p["g_kv"], p["wukv"], p["wukt"],
      p["sc_conv_w"], p["w_sc_o"], p["cf_conv_w"], p["cf_conv_b"], p["cf_ln_g"], p["cf_ln_b"], p["w_cf_o"],
      sc_hist_t, cf_hist_t)


def _flash_kernel(q_ref, k_ref, v_ref, o_ref, m_scr, l_scr, acc_scr, *, tq, v_dim):
    qi = pl.program_id(2)
    reps = tq // LANES
    m_scr[...] = jnp.full(m_scr.shape, NEG_INF, F32)
    l_scr[...] = jnp.zeros(l_scr.shape, F32)
    acc_scr[...] = jnp.zeros(acc_scr.shape, F32)

    def step(j, masked):
        start = pl.multiple_of(j * tq, tq)
        vt = v_ref[0, pl.ds(start, tq), :]
        for hh in range(2):
            q = q_ref[0, :, hh * LANES:(hh + 1) * LANES]
            kt = k_ref[0, pl.ds(start, tq), hh * LANES:(hh + 1) * LANES]
            s = _dot_nt(q, kt)
            if masked:
                r = lax.broadcasted_iota(jnp.int32, (tq, tq), 0)
                cidx = lax.broadcasted_iota(jnp.int32, (tq, tq), 1)
                s = jnp.where(cidx <= r, s, NEG_INF)
            m_prev = m_scr[hh]
            m_next = jnp.maximum(m_prev, jnp.max(s, axis=1, keepdims=True))
            p = jnp.exp2(s - jnp.tile(m_next, (1, reps)))
            alpha = jnp.exp2(m_prev - m_next)
            l_scr[hh] = alpha * l_scr[hh] + jnp.sum(p, axis=1, keepdims=True)
            m_scr[hh] = m_next
            acc_scr[hh] = alpha * acc_scr[hh] + jnp.dot(p.astype(BF16), vt, preferred_element_type=F32)

    def body(j, carry):
        step(j, False)
        return carry

    lax.fori_loop(0, qi, body, 0)
    step(qi, True)
    lane = lax.broadcasted_iota(jnp.int32, (tq, LANES), 1)
    o_ref[0] = jnp.where(lane < v_dim, acc_scr[0] / l_scr[0], acc_scr[1] / l_scr[1]).astype(o_ref.dtype)


def _flash_attention(q, k, v, dims, tq):
    b, s, hp = q.shape
    nh = dims["n_heads"]
    assert 2 * dims["v_dim"] == LANES and nh % 2 == 0
    kern = functools.partial(_flash_kernel, tq=tq, v_dim=dims["v_dim"])
    return pl.pallas_call(
        kern,
        grid=(b, nh // 2, s // tq),
        in_specs=[pl.BlockSpec((1, tq, 2 * LANES), lambda bi, hi, i: (bi, i, hi)),
                  pl.BlockSpec((1, s, 2 * LANES), lambda bi, hi, i: (bi, 0, hi)),
                  pl.BlockSpec((1, s, LANES), lambda bi, hi, i: (bi, 0, hi))],
        out_specs=pl.BlockSpec((1, tq, LANES), lambda bi, hi, i: (bi, i, hi)),
        out_shape=jax.ShapeDtypeStruct((b, s, nh * dims["v_dim"]), BF16),
        scratch_shapes=[pltpu.VMEM((2, tq, LANES), F32)] * 3,
        compiler_params=_cparams(3),
        name="mla_flash_attention",
    )(q, k, v)


def _decode_kernel(pt_ref, qlat_ref, qr_ref, cnew_ref, krnew_ref, ckv_hbm, ckr_hbm, o_ref,
                   cbuf, rbuf, sems, *, layer, n_seq, n_pages, page):
    def page_copies(b, slot, pi):
        pg = pt_ref[b, pi]
        off = pl.multiple_of(pi * page, page)
        return (pltpu.make_async_copy(ckv_hbm.at[layer, pg], cbuf.at[slot, pl.ds(off, page)], sems.at[0, slot]),
                pltpu.make_async_copy(ckr_hbm.at[layer, pg], rbuf.at[slot, :, pl.ds(off, page)], sems.at[1, slot]))

    def start_seq(b, slot):
        def body(pi, carry):
            for cpy in page_copies(b, slot, pi):
                cpy.start()
            return carry
        lax.fori_loop(0, n_pages, body, 0, unroll=math.gcd(n_pages, 8))

    def wait_seq(b, slot):
        def body(pi, carry):
            for cpy in page_copies(b, slot, pi):
                cpy.wait()
            return carry
        lax.fori_loop(0, n_pages, body, 0, unroll=math.gcd(n_pages, 8))

    start_seq(0, 0)

    def body(b, carry):
        slot = b % 2

        @pl.when(b + 1 < n_seq)
        def _():
            start_seq(b + 1, 1 - slot)

        q = qlat_ref[b]
        qr = qr_ref[b]
        cnew = cnew_ref[b]
        s_self = (jnp.sum(q * cnew, axis=-1, keepdims=True)
                  + jnp.sum(qr * krnew_ref[b], axis=-1, keepdims=True))
        wait_seq(b, slot)
        qb = q.astype(BF16)
        qrb = qr.astype(BF16)
        span = (n_pages * page) // DECODE_SPLITS
        m = s_self
        den = jnp.ones_like(s_self)
        acc = jnp.broadcast_to(cnew, q.shape)
        for i in range(DECODE_SPLITS):
            cb = cbuf[slot, i * span:(i + 1) * span, :].astype(BF16)
            kb = rbuf[slot, :, i * span:(i + 1) * span].astype(BF16)
            s = _dot_nt(qb, cb) + jnp.dot(qrb, kb, preferred_element_type=F32)
            m_new = jnp.maximum(m, jnp.max(s, axis=-1, keepdims=True))
            alpha = jnp.exp(m - m_new)
            pr = jnp.exp(s - m_new)
            den = alpha * den + jnp.sum(pr, axis=-1, keepdims=True)
            acc = alpha * acc + jnp.dot(pr.astype(BF16), cb, preferred_element_type=F32)
            m = m_new
        o_ref[b] = acc / den
        return carry

    lax.fori_loop(0, n_seq, body, 0)


def _decode_attention(page_table, qlat, qr, cnew, krnew, cache_kv, cache_kr_t, layer):
    n_seq, nh, r = qlat.shape
    n_pages = page_table.shape[1]
    page = cache_kv.shape[2]
    rope = cache_kr_t.shape[2]
    past = n_pages * page
    assert 2 * past * (r + rope) * 4 + past * r * 2 < VMEM_LIMIT - (8 << 20), "past does not fit in VMEM"
    kern = functools.partial(_decode_kernel, layer=layer, n_seq=n_seq, n_pages=n_pages, page=page)
    vm = pl.BlockSpec(memory_space=pltpu.VMEM)
    return pl.pallas_call(
        kern,
        grid_spec=pltpu.PrefetchScalarGridSpec(
            num_scalar_prefetch=1,
            grid=(1,),
            in_specs=[vm, vm, vm, vm, pl.BlockSpec(memory_space=pl.ANY), pl.BlockSpec(memory_space=pl.ANY)],
            out_specs=vm,
            scratch_shapes=[pltpu.VMEM((2, past, r), F32), pltpu.VMEM((2, rope, past), F32),
                            pltpu.SemaphoreType.DMA((2, 2))]),
        out_shape=jax.ShapeDtypeStruct((n_seq, nh, r), F32),
        compiler_params=_cparams(1),
        name="paged_latent_attention",
    )(page_table, qlat, qr, cnew, krnew, cache_kv, cache_kr_t)


def _ov_kernel(o_ref, w_ref, out_ref, *, n_heads):
    for i in range(n_heads):
        out_ref[i] = _dot(o_ref[i], w_ref[i])


def _latent_to_value(olat_t, wuv_h):
    nh, n, _ = olat_t.shape
    return pl.pallas_call(
        functools.partial(_ov_kernel, n_heads=nh),
        out_shape=jax.ShapeDtypeStruct((nh, n, wuv_h.shape[2]), F32),
        name="latent_to_value",
    )(olat_t, wuv_h)


def _stage_c_kernel(x_ref, o_ref, g0_ref, mbc_ref, wmo_ref, wout_ref, gpost_ref, y_ref):
    br_a = _dot(o_ref[...], wmo_ref[...])
    merged = g0_ref[...] * br_a + mbc_ref[...]
    mixed = _dot(merged, wout_ref[...])
    y_ref[...] = x_ref[...] + _rms(mixed, gpost_ref[...])


def _stage_c(x, o, g0, mbc, p, tm):
    n, d = x.shape
    row = lambda w: pl.BlockSpec((tm, w), lambda i: (i, 0))
    consts = [p["w_mla_o"], p["w_out"], p["g_mix_post"]]
    return pl.pallas_call(
        _stage_c_kernel,
        grid=(n // tm,),
        in_specs=[row(d), row(o.shape[1]), row(d), row(d)] + [_const_spec(a.shape) for a in consts],
        out_specs=row(d),
        out_shape=jax.ShapeDtypeStruct((n, d), F32),
        compiler_params=_cparams(1),
        name="merge_out",
    )(x, o, g0, mbc, *consts)


def _mem_kv_kernel(m_ref, g_ref, wk_ref, wv_ref, k_ref, v_ref):
    h = _rms(m_ref[...], g_ref[...]).astype(BF16)
    k_ref[...] = jnp.dot(h, wk_ref[...], preferred_element_type=F32)
    v_ref[...] = jnp.dot(h, wv_ref[...], preferred_element_type=F32)


def _mem_kv(mem, p, tm):
    n, d = mem.shape
    row = pl.BlockSpec((tm, d), lambda i: (i, 0))
    consts = [p["g_mem_src"], p["w_mk"], p["w_mv"]]
    return pl.pallas_call(
        _mem_kv_kernel,
        grid=(n // tm,),
        in_specs=[row] + [_const_spec(a.shape) for a in consts],
        out_specs=[row, row],
        out_shape=[jax.ShapeDtypeStruct((n, d), F32)] * 2,
        compiler_params=_cparams(1),
        name="mem_kv",
    )(mem, *consts)


def _mem_heads(q, k, v, n_heads, hd):
    outs = []
    for i in range(n_heads):
        sl = slice(i * hd, (i + 1) * hd)
        s = _dot_nt(q[:, sl], k[:, sl])
        e = jnp.exp(s - jnp.max(s, axis=-1, keepdims=True))
        o = jnp.dot(e.astype(BF16), v[:, sl], preferred_element_type=F32)
        outs.append(o / jnp.sum(e, axis=-1, keepdims=True))
    return jnp.concatenate(outs, axis=-1)


def _stage_d_prompt_kernel(x_ref, k_ref, v_ref, gpre_ref, wmq_ref, wmo_ref, gpost_ref, y_ref,
                           *, n_heads, hd, scale):
    x = x_ref[0]
    h = _rms(x, gpre_ref[...])
    q = (_dot(h, wmq_ref[...]) * scale).astype(BF16)
    o = _mem_heads(q, k_ref[0].astype(BF16), v_ref[0].astype(BF16), n_heads, hd)
    y_ref[0] = x + _rms(_dot(o, wmo_ref[...]), gpost_ref[...])


def _stage_d_prompt(x, mk, mv, p, dims, tm):
    b, s, d = x.shape
    nm = mk.shape[1]
    kern = functools.partial(_stage_d_prompt_kernel, n_heads=dims["mem_heads"], hd=dims["mem_hd"],
                             scale=dims["mem_scale"])
    row = pl.BlockSpec((1, tm, d), lambda bi, i: (bi, i, 0))
    kv = pl.BlockSpec((1, nm, d), lambda bi, i: (bi, 0, 0))
    consts = [p["g_mem_pre"], p["w_mq"], p["w_mo"], p["g_mem_post"]]
    return pl.pallas_call(
        kern,
        grid=(b, s // tm),
        in_specs=[row, kv, kv] + [_const_spec(a.shape) for a in consts],
        out_specs=row,
        out_shape=jax.ShapeDtypeStruct((b, s, d), F32),
        compiler_params=_cparams(2),
        name="mem_attention_prompt",
    )(x, mk, mv, *consts)


def _mem_q_sample_kernel(x_ref, gpre_ref, wmq_ref, q_ref, *, scale):
    q_ref[...] = _dot(_rms(x_ref[...], gpre_ref[...]), wmq_ref[...]) * scale


def _mem_q_sample(x, p, dims):
    return pl.pallas_call(
        functools.partial(_mem_q_sample_kernel, scale=dims["mem_scale"]),
        out_shape=jax.ShapeDtypeStruct(x.shape, F32),
        name="mem_query_sample",
    )(x, p["g_mem_pre"], p["w_mq"])


def _mem_attn_sample_kernel(q_ref, k_ref, v_ref, o_ref):
    nm, nh, hd = k_ref.shape[2:]
    q = q_ref[0]
    q2 = jnp.concatenate([q, q], axis=0)
    kb = k_ref[0, 0].reshape(nm // 2, 2 * nh, hd)
    vb = v_ref[0, 0].reshape(nm // 2, 2 * nh, hd)
    s = jnp.sum(kb * q2[None], axis=-1, keepdims=True)
    m2 = jnp.max(s, axis=0, keepdims=True)
    m = jnp.maximum(m2[:, :nh], m2[:, nh:])
    e = jnp.exp(s - jnp.concatenate([m, m], axis=1))
    den = jnp.sum(e, axis=0)
    o = jnp.sum(e * vb, axis=0)
    o_ref[0] = (o[:nh] + o[nh:]) / (den[:nh] + den[nh:])


def _mem_attn_sample(q, mk, mv, layer):
    n, nh, hd = q.shape
    nm = mk.shape[2]
    qs = pl.BlockSpec((1, nh, hd), lambda bi: (bi, 0, 0))
    kv = pl.BlockSpec((1, 1, nm, nh, hd), lambda bi: (layer, bi, 0, 0, 0))
    return pl.pallas_call(
        _mem_attn_sample_kernel,
        grid=(n,),
        in_specs=[qs, kv, kv],
        out_specs=qs,
        out_shape=jax.ShapeDtypeStruct((n, nh, hd), F32),
        compiler_params=_cparams(1),
        name="mem_attention_sample",
    )(q, mk, mv)


def _ffn_chunks(d_ff):
    n = d_ff // LANES
    for parts in (2, 1):
        if n % parts == 0:
            return parts, d_ff // parts
    return 1, d_ff


def _stage_e_prompt_kernel(x_ref, gpre_ref, wup_ref, cw_ref, wdown_ref, gpost_ref, y_ref, st_ref,
                           carry, ubuf, *, tm, d_ff, k_w, parts, ch):
    i = pl.program_id(1)

    @pl.when(i == 0)
    def _():
        carry[...] = jnp.zeros(carry.shape, F32)

    x = x_ref[0]
    h = _rms(x, gpre_ref[...]).astype(BF16)

    def conv(c0):
        u = jnp.dot(h, wup_ref[:, c0:c0 + ch], preferred_element_type=F32)
        ubuf[0:SUBLANES, :] = carry[:, c0:c0 + ch]
        ubuf[SUBLANES:SUBLANES + tm, :] = u
        uc = cw_ref[k_w - 1:k_w, c0:c0 + ch] * u
        for t in range(k_w - 1):
            off = SUBLANES - (k_w - 1) + t
            uc = uc + cw_ref[t:t + 1, c0:c0 + ch] * ubuf[off:off + tm, :]
        carry[:, c0:c0 + ch] = ubuf[tm:tm + SUBLANES, :]
        return uc

    f = None
    for pi in range(parts):
        gt = conv(pi * ch)
        val = conv(d_ff + pi * ch)
        part = _dot(_gelu_tanh(gt) * val, wdown_ref[pi * ch:(pi + 1) * ch, :])
        f = part if f is None else f + part
    y_ref[0] = x + _rms(f, gpost_ref[...])
    st_ref[0] = carry[SUBLANES - (k_w - 1):SUBLANES, :]


def _stage_e_prompt(x, p, dims, tm):
    b, s, d = x.shape
    d_ff = dims["d_ff"]
    k_w = dims["ffn_k"]
    parts, ch = _ffn_chunks(d_ff)
    kern = functools.partial(_stage_e_prompt_kernel, tm=tm, d_ff=d_ff, k_w=k_w, parts=parts, ch=ch)
    row = pl.BlockSpec((1, tm, d), lambda bi, i: (bi, i, 0))
    consts = [p["g_ffn_pre"], p["w_up"], p["ffn_conv_w"], p["w_down"], p["g_ffn_post"]]
    return pl.pallas_call(
        kern,
        grid=(b, s // tm),
        in_specs=[row] + [_const_spec(a.shape) for a in consts],
        out_specs=[row, pl.BlockSpec((1, k_w - 1, 2 * d_ff), lambda bi, i: (bi, 0, 0))],
        out_shape=[jax.ShapeDtypeStruct((b, s, d), F32), jax.ShapeDtypeStruct((b, k_w - 1, 2 * d_ff), F32)],
        scratch_shapes=[pltpu.VMEM((SUBLANES, 2 * d_ff), F32), pltpu.VMEM((SUBLANES + tm, ch), F32)],
        compiler_params=_cparams(2),
        name="conv_ffn_prompt",
    )(x, *consts)


def _stage_e_sample_kernel(x_ref, o_ref, wmo_ref, gmpost_ref, hist_ref, gpre_ref, wup_ref, cw_ref, wdown_ref,
                           gpost_ref, y_ref, u_ref, *, d_ff, k_w):
    x = x_ref[...] + _rms(_dot(o_ref[...], wmo_ref[...]), gmpost_ref[...])
    h = _rms(x, gpre_ref[...])
    u = _dot(h, wup_ref[...])
    u_ref[...] = u
    uc = cw_ref[k_w - 1:k_w, :] * u
    for t in range(k_w - 1):
        uc = uc + cw_ref[t:t + 1, :] * hist_ref[t]
    f = _dot(_gelu_tanh(uc[:, :d_ff]) * uc[:, d_ff:], wdown_ref[...])
    y_ref[...] = x + _rms(f, gpost_ref[...])


def _stage_e_sample(x, o_mem, hist_t, p, dims):
    n, d = x.shape
    d_ff = dims["d_ff"]
    kern = functools.partial(_stage_e_sample_kernel, d_ff=d_ff, k_w=dims["ffn_k"])
    return pl.pallas_call(
        kern,
        out_shape=[jax.ShapeDtypeStruct((n, d), F32), jax.ShapeDtypeStruct((n, 2 * d_ff), F32)],
        compiler_params=pltpu.CompilerParams(vmem_limit_bytes=VMEM_LIMIT),
        name="conv_ffn_sample",
    )(x, o_mem, p["w_mo"], p["g_mem_post"], hist_t, p["g_ffn_pre"], p["w_up"], p["ffn_conv_w"], p["w_down"],
      p["g_ffn_post"])


def _rope_table(pos, rope_dim, nope_dim):
    half = rope_dim // 2
    inv = ROPE_THETA ** (-jnp.arange(half, dtype=F32) / half)
    ang = pos.astype(F32)[:, None] * inv[None, :]
    cos, sin = jnp.cos(ang), jnp.sin(ang)
    t = pos.shape[0]
    pad = LANES - nope_dim - rope_dim
    tc = jnp.concatenate([jnp.ones((t, nope_dim), F32), cos, cos, jnp.zeros((t, pad), F32)], axis=1)
    ts = jnp.concatenate([jnp.zeros((t, nope_dim), F32), sin, sin, jnp.zeros((t, pad), F32)], axis=1)
    return jnp.concatenate([tc, ts], axis=1)


def _rot_cols(w, half):
    return jnp.concatenate([-w[..., half:], w[..., :half]], axis=-1)


def _prep_layer(l, dims, w):
    d = dims["d_model"]
    nh, nope, rope, vd = dims["n_heads"], dims["nope_dim"], dims["rope_dim"], dims["v_dim"]
    qr, kvr, sc_w, cf_w = dims["q_rank"], dims["kv_rank"], dims["sc_w"], dims["cf_w"]
    half = rope // 2
    pad = LANES - nope - rope
    w_in = w["w_in"][l]
    sizes = [qr, kvr, rope, sc_w, sc_w, sc_w, cf_w, cf_w, 3 * d]
    offs = [0]
    for sz in sizes:
        offs.append(offs[-1] + sz)
    seg = [w_in[:, offs[i]:offs[i + 1]] for i in range(len(sizes))]
    zeros = lambda n: jnp.zeros((d, n), F32)
    kr_pos = jnp.concatenate([zeros(nope), seg[2], zeros(pad)], axis=1)
    kr_rot = jnp.concatenate([zeros(nope), _rot_cols(seg[2], half), zeros(pad)], axis=1)
    wa = jnp.concatenate([seg[0], seg[1], kr_pos, kr_rot] + seg[3:], axis=1).astype(BF16)

    wuq = w["w_uq"][l].reshape(qr, nh, nope + rope)
    zq = lambda n: jnp.zeros((qr, nh, n), F32)
    uq_pos = jnp.concatenate([wuq, zq(pad)], axis=2).reshape(qr, nh * LANES)
    uq_rot = jnp.concatenate([zq(nope), _rot_cols(wuq[..., nope:], half), zq(pad)], axis=2).reshape(qr, nh * LANES)
    wuq2 = jnp.concatenate([uq_pos, uq_rot], axis=1).astype(BF16)

    wuk = w["w_uk"][l]
    wuv = w["w_uv"][l]
    uk_pad = jnp.concatenate([wuk, jnp.zeros((kvr, nh, LANES - nope), F32)], axis=2).reshape(kvr, nh * LANES)
    wukv = jnp.concatenate([uk_pad, wuv.reshape(kvr, nh * vd)], axis=1).astype(BF16)
    wukt = jnp.concatenate([wuk.transpose(1, 2, 0), jnp.zeros((nh, LANES - nope, kvr), F32)], axis=1).astype(BF16)
    wuv_h = wuv.transpose(1, 0, 2).astype(BF16)

    row = lambda name: w[name][l][None, :]
    return dict(
        wa=wa, wuq=wuq2, wukv=wukv, wukt=wukt, wuv_h=wuv_h,
        g_mix_pre=row("g_mix_pre"), g_q=row("g_q"), g_kv=row("g_kv"),
        sc_conv_w=w["sc_conv_w"][l], w_sc_o=w["w_sc_o"][l].astype(BF16),
        cf_conv_w=w["cf_conv_w"][l], cf_conv_b=row("cf_conv_b"), cf_ln_g=row("cf_ln_g"), cf_ln_b=row("cf_ln_b"),
        w_cf_o=w["w_cf_o"][l].astype(BF16), w_mla_o=w["w_mla_o"][l].astype(BF16), w_out=w["w_out"][l].astype(BF16),
        g_mix_post=row("g_mix_post"), g_mem_pre=row("g_mem_pre"), g_mem_src=row("g_mem_src"),
        w_mq=w["w_mq"][l].astype(BF16), w_mk=w["w_mk"][l].astype(BF16), w_mv=w["w_mv"][l].astype(BF16),
        w_mo=w["w_mo"][l].astype(BF16), g_mem_post=row("g_mem_post"), g_ffn_pre=row("g_ffn_pre"),
        w_up=w["w_up"][l].astype(BF16), ffn_conv_w=w["ffn_conv_w"][l], w_down=w["w_down"][l].astype(BF16),
        g_ffn_post=row("g_ffn_post"))


def _pick_tile(n, pref):
    t = min(n, pref)
    while n % t:
        t //= 2
    return t


def kernel(x_prompt, x_sample, mem_prompt, cache_kv_latent, cache_k_rope, cache_mem_k, cache_mem_v, state_sconv, state_conformer, state_ffn, page_table, g_mix_pre, w_in, g_q, w_uq, g_kv, w_uk, w_uv, w_mla_o, sc_conv_w, w_sc_o, cf_conv_w, cf_conv_b, cf_ln_g, cf_ln_b, w_cf_o, w_out, g_mix_post, g_mem_pre, g_mem_src, w_mq, w_mk, w_mv, w_mo, g_mem_post, g_ffn_pre, w_up, ffn_conv_w, w_down, g_ffn_post):
    w = dict(g_mix_pre=g_mix_pre, w_in=w_in, g_q=g_q, w_uq=w_uq, g_kv=g_kv, w_uk=w_uk, w_uv=w_uv,
             w_mla_o=w_mla_o, sc_conv_w=sc_conv_w, w_sc_o=w_sc_o, cf_conv_w=cf_conv_w, cf_conv_b=cf_conv_b,
             cf_ln_g=cf_ln_g, cf_ln_b=cf_ln_b, w_cf_o=w_cf_o, w_out=w_out, g_mix_post=g_mix_post,
             g_mem_pre=g_mem_pre, g_mem_src=g_mem_src, w_mq=w_mq, w_mk=w_mk, w_mv=w_mv, w_mo=w_mo,
             g_mem_post=g_mem_post, g_ffn_pre=g_ffn_pre, w_up=w_up, ffn_conv_w=ffn_conv_w, w_down=w_down,
             g_ffn_post=g_ffn_post)
    b, s, d = x_prompt.shape
    bd, t_dec, _ = x_sample.shape
    assert t_dec == 1, "sample group is one token per sequence"
    depth = w_in.shape[0]
    n_mem = mem_prompt.shape[1]
    _, kv_rank, n_heads, nope = w_uk.shape
    rope = cache_k_rope.shape[3]
    page = cache_kv_latent.shape[2]
    n_pages = page_table.shape[1]
    mem_heads, mem_hd = cache_mem_k.shape[3], cache_mem_k.shape[4]
    dims = dict(
        d_model=d, n_heads=n_heads, nope_dim=nope, rope_dim=rope, v_dim=w_uv.shape[3], q_rank=g_q.shape[1],
        kv_rank=kv_rank, sc_w=state_sconv.shape[3], cf_w=state_conformer.shape[3], sc_k=sc_conv_w.shape[1],
        cf_k=cf_conv_w.shape[1], ffn_k=ffn_conv_w.shape[1], d_ff=w_down.shape[1],
        mem_heads=mem_heads, mem_hd=mem_hd, mla_scale=1.0 / math.sqrt(nope + rope),
        mem_scale=1.0 / math.sqrt(mem_hd))
    dims["cols"] = _Cols(dims["q_rank"], kv_rank, dims["sc_w"], dims["cf_w"], d)
    assert nope + rope <= LANES and dims["cf_k"] - 1 <= CF_HALO and dims["sc_k"] - 1 <= SUBLANES

    tm = _pick_tile(s, 512)
    tq = _pick_tile(s, FLASH_TILE)
    assert tm >= CF_HALO
    past_len = n_pages * page
    cache_kr_t = jnp.swapaxes(cache_k_rope, 2, 3)
    tab_p = _rope_table(jnp.arange(s), rope, nope)
    tab_s = _rope_table(past_len + jnp.arange(t_dec), rope, nope)

    xp = x_prompt
    xs = x_sample.reshape(bd, d)
    mem2 = mem_prompt.reshape(b * n_mem, d)
    outs = {k: [] for k in ("c_p", "kr_p", "sc_p", "cf_p", "ffn_p", "mk_p", "mv_p",
                            "c_s", "kr_s", "sc_s", "cf_s", "ffn_s")}
    for l in range(depth):
        p = _prep_layer(l, dims, w)

        mk, mv = _mem_kv(mem2, p, _pick_tile(b * n_mem, 512))
        mk = mk.reshape(b, n_mem, d)
        mv = mv.reshape(b, n_mem, d)
        q, k, v, c, kr, g0, mbc, sc_st, cf_st = _stage_a_prompt(xp, tab_p, p, dims, tm)
        o = _flash_attention(q, k, v, dims, tq)
        x1 = _stage_c(xp.reshape(b * s, d), o.reshape(b * s, -1), g0.reshape(b * s, d), mbc.reshape(b * s, d),
                      p, _pick_tile(b * s, 512)).reshape(b, s, d)
        x2 = _stage_d_prompt(x1, mk, mv, p, dims, tm)
        xp, ffn_st = _stage_e_prompt(x2, p, dims, tm)
        outs["c_p"].append(c); outs["kr_p"].append(kr); outs["sc_p"].append(sc_st); outs["cf_p"].append(cf_st)
        outs["ffn_p"].append(ffn_st)
        outs["mk_p"].append(mk.reshape(b, n_mem, mem_heads, mem_hd))
        outs["mv_p"].append(mv.reshape(b, n_mem, mem_heads, mem_hd))

        sc_hist, cf_hist, ffn_hist = state_sconv[l], state_conformer[l], state_ffn[l]
        qabs, qr_s, c_s, kr_s, g0_s, mbc_s, scu_s, glu_s = _stage_a_sample(
            xs, tab_s, p, sc_hist.transpose(1, 0, 2), cf_hist.transpose(1, 0, 2), dims)
        olat = _decode_attention(page_table, qabs.transpose(1, 0, 2), qr_s.transpose(1, 0, 2),
                                 c_s[:, None, :], kr_s[:, None, :], cache_kv_latent, cache_kr_t, l)
        o_s = _latent_to_value(olat.transpose(1, 0, 2), p["wuv_h"]).transpose(1, 0, 2).reshape(bd, -1)
        x1s = _stage_c(xs, o_s, g0_s, mbc_s, p, bd)
        q_mem = _mem_q_sample(x1s, p, dims).reshape(bd, mem_heads, mem_hd)
        o_mem = _mem_attn_sample(q_mem, cache_mem_k, cache_mem_v, l).reshape(bd, d)
        xs, u_s = _stage_e_sample(x1s, o_mem, ffn_hist.transpose(1, 0, 2), p, dims)
        outs["c_s"].append(c_s[:, None, :]); outs["kr_s"].append(kr_s[:, None, :])
        outs["sc_s"].append(jnp.concatenate([sc_hist[:, 1:], scu_s[:, None, :]], axis=1))
        outs["cf_s"].append(jnp.concatenate([cf_hist[:, 1:], glu_s[:, None, :]], axis=1))
        outs["ffn_s"].append(jnp.concatenate([ffn_hist[:, 1:], u_s[:, None, :]], axis=1))

    st = lambda name: jnp.stack(outs[name])
    return (xp, xs.reshape(bd, t_dec, d),
            st("c_p"), st("kr_p"), st("sc_p"), st("cf_p"), st("ffn_p"), st("mk_p"), st("mv_p"),
            st("c_s"), st("kr_s"), st("sc_s"), st("cf_s"), st("ffn_s"))
```
